```python
import math
import jax, jax.numpy as jnp
from jax import lax
import numpy as np

D_MODEL = 1024
BATCH = 16
SEQ = 4096
DEPTH = 2

CHUNK = 64
QBLK = 128
MEM_LEN = 256
BRANCH_WIDTH = D_MODEL // 2
SB_HEAD_DIM = 64
SB_HEADS = BRANCH_WIDTH // SB_HEAD_DIM
DIFF_HEAD_DIM = 64
DIFF_HEADS = BRANCH_WIDTH // (2 * DIFF_HEAD_DIM)
MEM_HEAD_DIM = 128
MEM_HEADS = BRANCH_WIDTH // MEM_HEAD_DIM
N_BRANCHES = 3
IN_WIDTH = 7 * BRANCH_WIDTH + N_BRANCHES * D_MODEL
NUM_BUCKETS = 32
MAX_DISTANCE = 128
N_EXPERTS = 32
TOP_K = 4
D_FF = D_MODEL
SWIGLU_LIMIT = 7.0
SWIGLU_ALPHA = 1.702
LN_EPS = 1e-5
RMS_EPS = 1e-5
DEEPNORM_ALPHA = (2 * DEPTH) ** 0.25
DEEPNORM_BETA = (8 * DEPTH) ** -0.25

kernel_name = "hybrid_stickbreak_diffattn_memxattn_moe_deepnorm"

F32 = jnp.float32


def _layer_norm(x, g, b):
    xf = x.astype(F32)
    mu = jnp.mean(xf, axis=-1, keepdims=True)
    xc = xf - mu
    var = jnp.mean(xc * xc, axis=-1, keepdims=True)
    return (xc * lax.rsqrt(var + LN_EPS) * g.astype(F32) + b.astype(F32)).astype(x.dtype)


def _t5_bucket(rel):
    half = NUM_BUCKETS // 2
    max_exact = half // 2
    n = jnp.abs(rel)
    nf = jnp.maximum(n, 1).astype(F32)
    large = max_exact + (jnp.log(nf / max_exact) / math.log(MAX_DISTANCE / max_exact)
                         * (half - max_exact)).astype(jnp.int32)
    large = jnp.minimum(large, half - 1)
    return jnp.where(rel > 0, half, 0) + jnp.where(n < max_exact, n, large)


def _query_blocks(q):
    B, S, H, d = q.shape
    return q.astype(F32).reshape(B, S // QBLK, QBLK, H, d).transpose(1, 0, 3, 2, 4)


def _unblock(o):
    nb, B, Q, H, e = o.shape
    return o.transpose(1, 0, 2, 3, 4).reshape(B, nb * Q, H, e)


def _stick_breaking_attention(q, k, v):
    B, S = q.shape[:2]
    nb = S // QBLK
    kf, vf = k.astype(F32), v.astype(F32)
    kpos = jnp.arange(S)
    scale = SB_HEAD_DIM ** -0.5

    def block(args):
        qblk, i = args
        t = i * QBLK + jnp.arange(QBLK)
        z = jnp.einsum('bhqd,bshd->bhqs', qblk, kf) * scale
        earlier = kpos[None, :] < t[:, None]
        log_1mb = jnp.where(earlier, jax.nn.log_sigmoid(-z), 0.0)
        between = lax.cumsum(log_1mb, axis=3, reverse=True) - log_1mb
        a = jnp.where(earlier, jnp.exp(jax.nn.log_sigmoid(z) + between), 0.0)
        return jnp.einsum('bhqs,bshd->bqhd', a, vf)

    o = _unblock(lax.map(block, (_query_blocks(q), jnp.arange(nb))))
    return o.reshape(B, S, BRANCH_WIDTH).astype(q.dtype)


def _diff_attention(q, k, v, lam, lam_init, subln_g, rel_bias):
    B, S = q.shape[:2]
    nb = S // QBLK
    kf, vf = k.astype(F32), v.astype(F32)
    kpos = jnp.arange(S)
    table = rel_bias.astype(F32)
    scale = DIFF_HEAD_DIM ** -0.5

    def block(args):
        qblk, i = args
        t = i * QBLK + jnp.arange(QBLK)
        bias = table[_t5_bucket(kpos[None, :] - t[:, None])].transpose(2, 0, 1)
        logits = jnp.einsum('bhqd,bshd->bhqs', qblk, kf) * scale + bias
        visible = (kpos[None, :] // CHUNK) <= (t[:, None] // CHUNK)
        p = jax.nn.softmax(jnp.where(visible, logits, -jnp.inf), axis=-1)
        p = p.reshape(B, DIFF_HEADS, 2, QBLK, S)
        a = p[:, :, 0] - lam * p[:, :, 1]
        return jnp.einsum('bhqs,bshe->bqhe', a, vf)

    o = _unblock(lax.map(block, (_query_blocks(q), jnp.arange(nb))))
    o = o * lax.rsqrt(jnp.mean(o * o, axis=-1, keepdims=True) + RMS_EPS) * subln_g.astype(F32)
    return (o * (1.0 - lam_init)).reshape(B, S, BRANCH_WIDTH).astype(q.dtype)


def _memory_attention(q, mem_k, mem_v):
    B, S = q.shape[:2]
    logits = jnp.einsum('bshd,bmhd->bhsm', q.astype(F32), mem_k.astype(F32)) * MEM_HEAD_DIM ** -0.5
    p = jax.nn.softmax(logits, axis=-1)
    o = jnp.einsum('bhsm,bmhd->bshd', p, mem_v.astype(F32))
    return o.reshape(B, S, BRANCH_WIDTH).astype(q.dtype)


def _mixer(x, mem, w_in, b_gate, diff_lambda, diff_subln_g, rel_bias, w_mem_kv, w_branch, w_out, layer_idx):
    B, S, _ = x.shape
    W = BRANCH_WIDTH
    proj = x @ w_in
    sb_q, sb_k, sb_v, df_q, df_k, df_v, mem_q = [proj[..., i * W:(i + 1) * W] for i in range(7)]
    gates = jax.nn.sigmoid((proj[..., 7 * W:] + b_gate).astype(F32)).astype(x.dtype)
    gates = gates.reshape(B, S, N_BRANCHES, D_MODEL)

    hd = (B, S, SB_HEADS, SB_HEAD_DIM)
    y_sb = _stick_breaking_attention(sb_q.reshape(hd), sb_k.reshape(hd), sb_v.reshape(hd))

    lam_init = 0.8 - 0.6 * math.exp(-0.3 * layer_idx)
    lf = diff_lambda.astype(F32)
    lam = jnp.exp(jnp.sum(lf[0] * lf[1])) - jnp.exp(jnp.sum(lf[2] * lf[3])) + lam_init
    qk_shape = (B, S, 2 * DIFF_HEADS, DIFF_HEAD_DIM)
    y_df = _diff_attention(df_q.reshape(qk_shape), df_k.reshape(qk_shape),
                           df_v.reshape(B, S, DIFF_HEADS, 2 * DIFF_HEAD_DIM),
                           lam, lam_init, diff_subln_g, rel_bias)

    M = mem.shape[1]
    mem_kv = mem @ w_mem_kv
    mem_k = mem_kv[..., :W].reshape(B, M, MEM_HEADS, MEM_HEAD_DIM)
    mem_v = mem_kv[..., W:].reshape(B, M, MEM_HEADS, MEM_HEAD_DIM)
    y_mem = _memory_attention(mem_q.reshape(B, S, MEM_HEADS, MEM_HEAD_DIM), mem_k, mem_v)

    merged = (gates[:, :, 0] * (y_sb @ w_branch[0])
              + gates[:, :, 1] * (y_df @ w_branch[1])
              + gates[:, :, 2] * (y_mem @ w_branch[2]))
    return merged @ w_out


def _moe(x, router_w, router_b, w_gate_up, b_gate_up, w_down, b_down):
    B, S, D = x.shape
    xt = x.reshape(B * S, D)
    logits = (xt @ router_w).astype(F32) + router_b.astype(F32)
    top_val, top_idx = lax.top_k(logits, TOP_K)
    top_w = jax.nn.softmax(top_val, axis=-1)
    combine = jnp.einsum('tk,tke->et', top_w,
                         jax.nn.one_hot(top_idx, N_EXPERTS, dtype=F32)).astype(x.dtype)

    def expert(acc, p):
        wgu, bgu, wd, bd, c = p
        h = xt @ wgu + bgu
        gate = jnp.minimum(h[:, :D_FF], SWIGLU_LIMIT)
        up = jnp.clip(h[:, D_FF:], -SWIGLU_LIMIT, SWIGLU_LIMIT)
        y = ((up + 1.0) * (gate * jax.nn.sigmoid(SWIGLU_ALPHA * gate))) @ wd + bd
        return acc + c[:, None] * y, None

    out, _ = lax.scan(expert, jnp.zeros_like(xt), (w_gate_up, b_gate_up, w_down, b_down, combine))
    return out.reshape(B, S, D)


def setup_inputs(seed: int = 0) -> dict:
    key = jax.random.key(seed)
    ks = jax.random.split(key, 20)
    L = DEPTH
    W = BRANCH_WIDTH

    def nrm(k, shape, scale):
        return jax.random.normal(k, shape, F32) * scale

    return {
        "x": nrm(ks[0], (BATCH, SEQ, D_MODEL), 1.0),
        "mem": nrm(ks[1], (BATCH, MEM_LEN, D_MODEL), 1.0),
        "w_in": nrm(ks[2], (L, D_MODEL, IN_WIDTH), D_MODEL ** -0.5),
        "b_gate": nrm(ks[3], (L, N_BRANCHES * D_MODEL), 0.1),
        "diff_lambda": nrm(ks[4], (L, 4, DIFF_HEAD_DIM), 0.1),
        "diff_subln_g": 1.0 + nrm(ks[5], (L, 2 * DIFF_HEAD_DIM), 0.02),
        "rel_bias": nrm(ks[6], (NUM_BUCKETS, 2 * DIFF_HEADS), 0.5),
        "w_mem_kv": nrm(ks[7], (L, D_MODEL, 2 * W), D_MODEL ** -0.5),
        "w_branch": nrm(ks[8], (L, N_BRANCHES, W, D_MODEL), W ** -0.5),
        "w_out": nrm(ks[9], (L, D_MODEL, D_MODEL), D_MODEL ** -0.5 * DEEPNORM_BETA),
        "ln1_g": 1.0 + nrm(ks[10], (L, D_MODEL), 0.02),
        "ln1_b": nrm(ks[11], (L, D_MODEL), 0.02),
        "router_w": nrm(ks[12], (L, D_MODEL, N_EXPERTS), D_MODEL ** -0.5),
        "router_b": nrm(ks[13], (L, N_EXPERTS), 0.01),
        "w_gate_up": nrm(ks[14], (L, N_EXPERTS, D_MODEL, 2 * D_FF), D_MODEL ** -0.5),
        "b_gate_up": nrm(ks[15], (L, N_EXPERTS, 2 * D_FF), 0.02),
        "w_down": nrm(ks[16], (L, N_EXPERTS, D_FF, D_MODEL), D_FF ** -0.5 * DEEPNORM_BETA),
        "b_down": nrm(ks[17], (L, N_EXPERTS, D_MODEL), 0.02),
        "ln2_g": 1.0 + nrm(ks[18], (L, D_MODEL), 0.02),
        "ln2_b": nrm(ks[19], (L, D_MODEL), 0.02),
    }


def reference(x, mem, w_in, b_gate, diff_lambda, diff_subln_g, rel_bias, w_mem_kv, w_branch, w_out,
              ln1_g, ln1_b, router_w, router_b, w_gate_up, b_gate_up, w_down, b_down, ln2_g, ln2_b):
    for l in range(DEPTH):
        h = _mixer(x, mem, w_in[l], b_gate[l], diff_lambda[l], diff_subln_g[l], rel_bias,
                   w_mem_kv[l], w_branch[l], w_out[l], l)
        x = _layer_norm(DEEPNORM_ALPHA * x + h, ln1_g[l], ln1_b[l])
        f = _moe(x, router_w[l], router_b[l], w_gate_up[l], b_gate_up[l], w_down[l], b_down[l])
        x = _layer_norm(DEEPNORM_ALPHA * x + f, ln2_g[l], ln2_b[l])
    return x
```

```python
import functools
import math

import jax
import jax.numpy as jnp
from jax import lax
from jax.experimental import pallas as pl
from jax.experimental.pallas import tpu as pltpu

F32 = jnp.float32
BF16 = jnp.bfloat16

D_MODEL = 1024
DEPTH = 2
CHUNK = 64
BRANCH_WIDTH = D_MODEL // 2
SB_HEAD_DIM = 64
DIFF_HEAD_DIM = 64
DIFF_HEADS = BRANCH_WIDTH // (2 * DIFF_HEAD_DIM)
MEM_HEAD_DIM = 128
MEM_HEADS = BRANCH_WIDTH // MEM_HEAD_DIM
N_BRANCHES = 3
GATE_WIDTH = N_BRANCHES * D_MODEL
IN_WIDTH = 7 * BRANCH_WIDTH + GATE_WIDTH
NUM_BUCKETS = 32
MAX_DISTANCE = 128
N_EXPERTS = 32
TOP_K = 4
D_FF = D_MODEL
SWIGLU_LIMIT = 7.0
SWIGLU_ALPHA = 1.702
LN_EPS = 1e-5
RMS_EPS = 1e-5
DEEPNORM_ALPHA = (2 * DEPTH) ** 0.25

LANES = 128
SUBLANES = 8
NEG_BIG = -1e30

_GATE_BLK = 0
_SBQ_BLK = GATE_WIDTH // LANES
_SBK_BLK = _SBQ_BLK + 4
_SBV_BLK = _SBQ_BLK + 8
_DFQ_BLK = _SBQ_BLK + 12
_DFK_BLK = _SBQ_BLK + 16
_DFV_BLK = _SBQ_BLK + 20
_MEMQ_BLK = _SBQ_BLK + 24

VMEM_LIMIT = 56 * 1024 * 1024

ATT_TILE = 256
MOE_TILE = 256
TOK_TILE = 512


def _cparams(sem):
    return pltpu.CompilerParams(dimension_semantics=sem, vmem_limit_bytes=VMEM_LIMIT)


def _matmul_kernel(a_ref, w_ref, o_ref):
    a = a_ref[...].astype(BF16)
    o_ref[...] = jnp.dot(a, w_ref[...], preferred_element_type=F32).astype(o_ref.dtype)


def _matmul(a, w, tm, tn, name):
    m, k = a.shape
    n = w.shape[1]
    return pl.pallas_call(
        _matmul_kernel,
        out_shape=jax.ShapeDtypeStruct((m, n), BF16),
        grid=(m // tm, n // tn),
        in_specs=[pl.BlockSpec((tm, k), lambda i, j: (i, 0)),
                  pl.BlockSpec((k, tn), lambda i, j: (0, j))],
        out_specs=pl.BlockSpec((tm, tn), lambda i, j: (i, j)),
        compiler_params=_cparams(("parallel", "arbitrary")),
        name=name,
    )(a, w)


def _sb_kernel(q_ref, k_ref, v_ref, u_ref, o_ref, *, tq):
    i = pl.program_id(2)
    lane = lax.broadcasted_iota(jnp.int32, (tq, LANES), 1)
    qs = q_ref[0].astype(F32) * (SB_HEAD_DIM ** -0.5)
    q_heads = (jnp.where(lane < SB_HEAD_DIM, qs, 0.0).astype(BF16),
               jnp.where(lane >= SB_HEAD_DIM, qs, 0.0).astype(BF16))
    u = u_ref[...]
    row = lax.broadcasted_iota(jnp.int32, (tq, tq), 0)
    col = lax.broadcasted_iota(jnp.int32, (tq, tq), 1)
    earlier = col < row

    def step(j, carry, masked):
        start = pl.multiple_of(j * tq, tq)
        kb = k_ref[0, pl.ds(start, tq), :]
        vb = v_ref[0, pl.ds(start, tq), :]
        out = []
        for h in range(2):
            c, acc = carry[2 * h], carry[2 * h + 1]
            z = lax.dot_general(q_heads[h], kb, (((1,), (1,)), ((), ())),
                                preferred_element_type=F32)
            lp = jnp.log(1.0 + jnp.exp(-jnp.abs(z)))
            ls = jnp.minimum(z, 0.0) - lp
            l1m = ls - z
            if masked:
                l1m = jnp.where(earlier, l1m, 0.0)
            between = jnp.dot(l1m.astype(BF16), u, preferred_element_type=F32)
            a = jnp.exp(ls + between + c)
            if masked:
                a = jnp.where(earlier, a, 0.0)
            acc = acc + jnp.dot(a.astype(BF16), vb, preferred_element_type=F32)
            c = c + between[:, 0:1] + l1m[:, 0:1]
            out += [c, acc]
        return tuple(out)

    zc = jnp.zeros((tq, 1), F32)
    za = jnp.zeros((tq, LANES), F32)
    carry = step(i, (zc, za, zc, za), True)
    carry = lax.fori_loop(0, i, lambda n, cr: step(i - 1 - n, cr, False), carry)
    o_ref[0] = jnp.where(lane < SB_HEAD_DIM, carry[1], carry[3]).astype(o_ref.dtype)


def _sb_attention(proj, u, tq):
    b, s, _ = proj.shape
    kern = functools.partial(_sb_kernel, tq=tq)
    return pl.pallas_call(
        kern,
        out_shape=jax.ShapeDtypeStruct((b, s, BRANCH_WIDTH), BF16),
        grid=(b, BRANCH_WIDTH // LANES, s // tq),
        in_specs=[pl.BlockSpec((1, tq, LANES), lambda bi, hp, i: (bi, i, _SBQ_BLK + hp)),
                  pl.BlockSpec((1, s, LANES), lambda bi, hp, i: (bi, 0, _SBK_BLK + hp)),
                  pl.BlockSpec((1, s, LANES), lambda bi, hp, i: (bi, 0, _SBV_BLK + hp)),
                  pl.BlockSpec((tq, tq), lambda bi, hp, i: (0, 0))],
        out_specs=pl.BlockSpec((1, tq, LANES), lambda bi, hp, i: (bi, i, hp)),
        compiler_params=_cparams(("parallel", "parallel", "arbitrary")),
        name="sb_attention",
    )(proj, proj, proj, u)


def _df_kernel(lam_ref, g_ref, q_ref, k_ref, v_ref, bias_ref, o_ref, *, tq, lam_init):
    i = pl.program_id(2)
    lane = lax.broadcasted_iota(jnp.int32, (tq, LANES), 1)
    qs = q_ref[0].astype(F32) * (DIFF_HEAD_DIM ** -0.5)
    q_maps = (jnp.where(lane < DIFF_HEAD_DIM, qs, 0.0).astype(BF16),
              jnp.where(lane >= DIFF_HEAD_DIM, qs, 0.0).astype(BF16))

    def step(j, carry, bias):
        start = pl.multiple_of(j * tq, tq)
        kb = k_ref[0, pl.ds(start, tq), :]
        vb = v_ref[0, pl.ds(start, tq), :]
        out = []
        for m in range(2):
            mx, den, acc = carry[3 * m], carry[3 * m + 1], carry[3 * m + 2]
            sc = lax.dot_general(q_maps[m], kb, (((1,), (1,)), ((), ())),
                                 preferred_element_type=F32)
            if bias is not None:
                sc = sc + bias[m]
            mx_new = jnp.maximum(mx, jnp.max(sc, axis=-1, keepdims=True))
            alpha = jnp.exp(mx - mx_new)
            p = jnp.exp(sc - mx_new)
            den = alpha * den + jnp.sum(p, axis=-1, keepdims=True)
            acc = alpha * acc + jnp.dot(p.astype(BF16), vb, preferred_element_type=F32)
            out += [mx_new, den, acc]
        return tuple(out)

    m0 = jnp.full((tq, 1), NEG_BIG, F32)
    d0 = jnp.zeros((tq, 1), F32)
    a0 = jnp.zeros((tq, LANES), F32)
    carry = (m0, d0, a0, m0, d0, a0)
    carry = step(i, carry, (bias_ref[0, 0], bias_ref[1, 0]))
    off = jnp.where(i == 0, NEG_BIG, 0.0).astype(F32)
    carry = step(jnp.maximum(i - 1, 0), carry, (bias_ref[0, 1] + off, bias_ref[1, 1] + off))
    carry = lax.fori_loop(0, jnp.maximum(i - 1, 0),
                          lambda n, cr: step(i - 2 - n, cr, None), carry)

    lf = lam_ref[...]
    lam = (jnp.exp(jnp.sum(lf[0:1] * lf[1:2], axis=-1, keepdims=True))
           - jnp.exp(jnp.sum(lf[2:3] * lf[3:4], axis=-1, keepdims=True)) + lam_init)
    o = carry[2] / carry[1] - lam * (carry[5] / carry[4])
    o = o * lax.rsqrt(jnp.mean(o * o, axis=-1, keepdims=True) + RMS_EPS) * g_ref[...]
    o_ref[0] = (o * (1.0 - lam_init)).astype(o_ref.dtype)


def _df_attention(proj, lam_rows, subln_g, bias, tq, lam_init):
    b, s, _ = proj.shape
    kern = functools.partial(_df_kernel, tq=tq, lam_init=lam_init)
    return pl.pallas_call(
        kern,
        out_shape=jax.ShapeDtypeStruct((b, s, BRANCH_WIDTH), BF16),
        grid=(b, DIFF_HEADS, s // tq),
        in_specs=[pl.BlockSpec((4, DIFF_HEAD_DIM), lambda bi, h, i: (0, 0)),
                  pl.BlockSpec((1, LANES), lambda bi, h, i: (0, 0)),
                  pl.BlockSpec((1, tq, LANES), lambda bi, h, i: (bi, i, _DFQ_BLK + h)),
                  pl.BlockSpec((1, s, LANES), lambda bi, h, i: (bi, 0, _DFK_BLK + h)),
                  pl.BlockSpec((1, s, LANES), lambda bi, h, i: (bi, 0, _DFV_BLK + h)),
                  pl.BlockSpec((2, 2, tq, tq), lambda bi, h, i: (h, 0, 0, 0))],
        out_specs=pl.BlockSpec((1, tq, LANES), lambda bi, h, i: (bi, i, h)),
        compiler_params=_cparams(("parallel", "parallel", "arbitrary")),
        name="diff_attention",
    )(lam_rows, subln_g, proj, proj, proj, bias)


def _t5_bucket(rel):
    half = NUM_BUCKETS // 2
    max_exact = half // 2
    n = jnp.abs(rel)
    nf = jnp.maximum(n, 1).astype(F32)
    large = max_exact + (jnp.log(nf / max_exact) / math.log(MAX_DISTANCE / max_exact)
                         * (half - max_exact)).astype(jnp.int32)
    large = jnp.minimum(large, half - 1)
    return jnp.where(rel > 0, half, 0) + jnp.where(n < max_exact, n, large)


def _bias_tiles(rel_bias, tq):
    table = rel_bias.astype(F32)
    r = jnp.arange(tq, dtype=jnp.int32)
    rel_diag = r[None, :] - r[:, None]
    far = table[_t5_bucket(jnp.full((), -(tq + 1), jnp.int32))]
    diag = table[_t5_bucket(rel_diag)] - far
    visible = (r[None, :] // CHUNK) <= (r[:, None] // CHUNK)
    diag = jnp.where(visible[..., None], diag, NEG_BIG)
    near = table[_t5_bucket(rel_diag - tq)] - far
    return jnp.stack([diag, near], axis=0).transpose(3, 0, 1, 2)


def _mem_kernel(q_ref, kv_ref, o_ref):
    for h in range(MEM_HEADS):
        lo = h * MEM_HEAD_DIM
        q = (q_ref[0, :, lo:lo + MEM_HEAD_DIM].astype(F32) * (MEM_HEAD_DIM ** -0.5)).astype(BF16)
        k = kv_ref[0, :, lo:lo + MEM_HEAD_DIM]
        v = kv_ref[0, :, BRANCH_WIDTH + lo:BRANCH_WIDTH + lo + MEM_HEAD_DIM]
        sc = lax.dot_general(q, k, (((1,), (1,)), ((), ())), preferred_element_type=F32)
        p = jnp.exp(sc - jnp.max(sc, axis=-1, keepdims=True))
        den = jnp.sum(p, axis=-1, keepdims=True)
        o = jnp.dot(p.astype(BF16), v, preferred_element_type=F32) / den
        o_ref[0, :, lo:lo + MEM_HEAD_DIM] = o.astype(o_ref.dtype)


def _mem_attention(proj, mem_kv, tq):
    b, s, _ = proj.shape
    m = mem_kv.shape[1]
    return pl.pallas_call(
        _mem_kernel,
        out_shape=jax.ShapeDtypeStruct((b, s, BRANCH_WIDTH), BF16),
        grid=(b, s // tq),
        in_specs=[pl.BlockSpec((1, tq, BRANCH_WIDTH),
                               lambda bi, i: (bi, i, _MEMQ_BLK * LANES // BRANCH_WIDTH)),
                  pl.BlockSpec((1, m, 2 * BRANCH_WIDTH), lambda bi, i: (bi, 0, 0))],
        out_specs=pl.BlockSpec((1, tq, BRANCH_WIDTH), lambda bi, i: (bi, i, 0)),
        compiler_params=_cparams(("parallel", "arbitrary")),
        name="mem_attention",
    )(proj, mem_kv)


def _layer_norm(r, g, b):
    mu = jnp.mean(r, axis=-1, keepdims=True)
    rc = r - mu
    var = jnp.mean(rc * rc, axis=-1, keepdims=True)
    return rc * lax.rsqrt(var + LN_EPS) * g + b


def _merge_kernel(x_ref, g0_ref, g1_ref, g2_ref, ysb_ref, ydf_ref, ymem_ref, bg_ref, wb_ref,
                  wo_ref, lng_ref, lnb_ref, rwh_ref, rwl_ref, rb_ref, xn_ref, tw_ref, ti_ref):
    merged = None
    for br, (g_ref, y_ref) in enumerate(((g0_ref, ysb_ref), (g1_ref, ydf_ref), (g2_ref, ymem_ref))):
        gate = jax.nn.sigmoid(g_ref[...].astype(F32) + bg_ref[br:br + 1, :])
        term = gate * jnp.dot(y_ref[...], wb_ref[br], preferred_element_type=F32)
        merged = term if merged is None else merged + term
    h = jnp.dot(merged.astype(BF16), wo_ref[...], preferred_element_type=F32)
    xn = _layer_norm(DEEPNORM_ALPHA * x_ref[...].astype(F32) + h, lng_ref[...], lnb_ref[...])
    xn_ref[...] = xn

    xh = xn.astype(BF16)
    xl = (xn - xh.astype(F32)).astype(BF16)
    logits = (jnp.dot(xh, rwh_ref[...], preferred_element_type=F32)
              + jnp.dot(xl, rwh_ref[...], preferred_element_type=F32)
              + jnp.dot(xh, rwl_ref[...], preferred_element_type=F32)) + rb_ref[...]
    tm = logits.shape[0]
    eid = lax.broadcasted_iota(jnp.int32, (tm, N_EXPERTS), 1).astype(F32)
    cur = logits
    vals, idxs = [], []
    for _ in range(TOP_K):
        best = jnp.max(cur, axis=-1, keepdims=True)
        idx = jnp.min(jnp.where(cur == best, eid, float(N_EXPERTS)), axis=-1, keepdims=True)
        vals.append(best)
        idxs.append(idx)
        cur = jnp.where(eid == idx, -jnp.inf, cur)
    exps = [jnp.exp(v - vals[0]) for v in vals]
    den = exps[0] + exps[1] + exps[2] + exps[3]
    lane = lax.broadcasted_iota(jnp.int32, (tm, LANES), 1)
    tw = jnp.zeros((tm, LANES), F32)
    ti = jnp.zeros((tm, LANES), F32)
    for k in range(TOP_K):
        tw = jnp.where(lane == k, exps[k] / den, tw)
        ti = jnp.where(lane == k, idxs[k], ti)
    tw_ref[...] = tw
    ti_ref[...] = ti.astype(jnp.int32)


def _merge(x, proj, ysb, ydf, ymem, b_gate, wb, wo, ln_g, ln_b, rwh, rwl, rb, tm):
    t = x.shape[0]
    tok = lambda blk: pl.BlockSpec((tm, blk[0]), lambda i, c=blk[1]: (i, c))
    full2 = lambda shape: pl.BlockSpec(shape, lambda i: (0, 0))
    return pl.pallas_call(
        _merge_kernel,
        out_shape=(jax.ShapeDtypeStruct((t, D_MODEL), F32),
                   jax.ShapeDtypeStruct((t, LANES), F32),
                   jax.ShapeDtypeStruct((t, LANES), jnp.int32)),
        grid=(t // tm,),
        in_specs=[tok((D_MODEL, 0)),
                  tok((D_MODEL, 0)), tok((D_MODEL, 1)), tok((D_MODEL, 2)),
                  tok((BRANCH_WIDTH, 0)), tok((BRANCH_WIDTH, 0)), tok((BRANCH_WIDTH, 0)),
                  full2((N_BRANCHES, D_MODEL)),
                  pl.BlockSpec((N_BRANCHES, BRANCH_WIDTH, D_MODEL), lambda i: (0, 0, 0)),
                  full2((D_MODEL, D_MODEL)),
                  full2((1, D_MODEL)), full2((1, D_MODEL)),
                  full2((D_MODEL, N_EXPERTS)), full2((D_MODEL, N_EXPERTS)),
                  full2((1, N_EXPERTS))],
        out_specs=(pl.BlockSpec((tm, D_MODEL), lambda i: (i, 0)),
                   pl.BlockSpec((tm, LANES), lambda i: (i, 0)),
                   pl.BlockSpec((tm, LANES), lambda i: (i, 0))),
        compiler_params=_cparams(("parallel",)),
        name="merge_ln_router",
    )(x, proj, proj, proj, ysb, ydf, ymem, b_gate, wb, wo, ln_g, ln_b, rwh, rwl, rb)


def _moe_kernel(te_ref, nv_ref, tok_ref, dst_ref, x_hbm, wgu_ref, bgu_ref, wd_ref, bd_ref,
                y_hbm, xbuf, ybuf, sem_in, sem_out, *, tm):
    g = pl.program_id(0)
    nv = nv_ref[g]

    @pl.when(nv > 0)
    def _():
        def gather(r, carry):
            t = tok_ref[0, 0, r]
            pltpu.make_async_copy(x_hbm.at[pl.ds(t, 1), :], xbuf.at[pl.ds(r, 1), :], sem_in).start()
            return carry

        lax.fori_loop(0, tm, gather, 0)
        pltpu.make_async_copy(x_hbm.at[pl.ds(0, tm), :], xbuf, sem_in).wait()

        xb = xbuf[...].astype(BF16)
        h = jnp.dot(xb, wgu_ref[0], preferred_element_type=F32) + bgu_ref[0]
        gate = jnp.minimum(h[:, :D_FF], SWIGLU_LIMIT)
        up = jnp.clip(h[:, D_FF:], -SWIGLU_LIMIT, SWIGLU_LIMIT)
        act = (up + 1.0) * (gate * jax.nn.sigmoid(SWIGLU_ALPHA * gate))
        ybuf[...] = jnp.dot(act.astype(BF16), wd_ref[0], preferred_element_type=F32) + bd_ref[0]

        def scatter(r, carry):
            d = dst_ref[0, 0, r]
            pltpu.make_async_copy(ybuf.at[pl.ds(r, 1), :], y_hbm.at[pl.ds(d, 1), :], sem_out).start()
            return carry

        lax.fori_loop(0, nv, scatter, 0)

        def wait_rows(n):
            def body(_, carry):
                pltpu.make_async_copy(ybuf.at[pl.ds(0, n), :], y_hbm.at[pl.ds(0, n), :], sem_out).wait()
                return carry
            return body

        lax.fori_loop(0, lax.shift_right_logical(nv, 3), wait_rows(SUBLANES), 0)
        lax.fori_loop(0, nv & (SUBLANES - 1), wait_rows(1), 0)


def _moe(xn, te, nv, tok, dst, wgu, bgu, wd, bd, tm):
    t = xn.shape[0]
    n_tiles = tok.shape[0]
    kern = functools.partial(_moe_kernel, tm=tm)
    grid_spec = pltpu.PrefetchScalarGridSpec(
        num_scalar_prefetch=2,
        grid=(n_tiles,),
        in_specs=[pl.BlockSpec((1, 1, tm), lambda g, te, nv: (g, 0, 0), memory_space=pltpu.SMEM),
                  pl.BlockSpec((1, 1, tm), lambda g, te, nv: (g, 0, 0), memory_space=pltpu.SMEM),
                  pl.BlockSpec(memory_space=pl.ANY),
                  pl.BlockSpec((1, D_MODEL, 2 * D_FF), lambda g, te, nv: (te[g], 0, 0)),
                  pl.BlockSpec((1, 1, 2 * D_FF), lambda g, te, nv: (te[g], 0, 0)),
                  pl.BlockSpec((1, D_FF, D_MODEL), lambda g, te, nv: (te[g], 0, 0)),
                  pl.BlockSpec((1, 1, D_MODEL), lambda g, te, nv: (te[g], 0, 0))],
        out_specs=pl.BlockSpec(memory_space=pl.ANY),
        scratch_shapes=[pltpu.VMEM((tm, D_MODEL), F32),
                        pltpu.VMEM((tm, D_MODEL), F32),
                        pltpu.SemaphoreType.DMA,
                        pltpu.SemaphoreType.DMA],
    )
    return pl.pallas_call(
        kern,
        out_shape=jax.ShapeDtypeStruct((TOP_K * t, D_MODEL), F32),
        grid_spec=grid_spec,
        compiler_params=_cparams(("arbitrary",)),
        name="moe_experts",
    )(te, nv, tok, dst, xn, wgu, bgu, wd, bd)


def _route(topi, t, tm):
    e_flat = topi[:, :TOP_K].T.reshape(-1)
    order = jnp.argsort(e_flat, stable=True).astype(jnp.int32)
    experts = jnp.arange(N_EXPERTS, dtype=jnp.int32)
    counts = jnp.sum((e_flat[:, None] == experts[None, :]).astype(jnp.int32), axis=0)
    off = jnp.cumsum(counts) - counts
    tiles = (counts + tm - 1) // tm
    tile_end = jnp.cumsum(tiles)
    tile_off = tile_end - tiles
    total = tile_end[-1]
    n_tiles = (TOP_K * t) // tm + N_EXPERTS
    g = jnp.arange(n_tiles, dtype=jnp.int32)
    te = jnp.minimum(jnp.searchsorted(tile_end, g, side="right"), N_EXPERTS - 1).astype(jnp.int32)
    r = jnp.arange(tm, dtype=jnp.int32)
    start = (g - tile_off[te]) * tm
    nv = jnp.where(g < total, jnp.clip(counts[te] - start, 0, tm), 0).astype(jnp.int32)
    local = start[:, None] + r[None, :]
    valid = r[None, :] < nv[:, None]
    src = jnp.clip(off[te][:, None] + local, 0, TOP_K * t - 1)
    f = order[src]
    tok = jnp.where(valid, f % t, 0).astype(jnp.int32)
    dst = jnp.where(valid, f, 0).astype(jnp.int32)
    return te, nv, tok.reshape(n_tiles, 1, tm), dst.reshape(n_tiles, 1, tm)


def _combine_kernel(xn_ref, tw_ref, y0_ref, y1_ref, y2_ref, y3_ref, lng_ref, lnb_ref, o_ref):
    tw = tw_ref[...]
    f = None
    for k, y_ref in enumerate((y0_ref, y1_ref, y2_ref, y3_ref)):
        term = tw[:, k:k + 1] * y_ref[...]
        f = term if f is None else f + term
    o_ref[...] = _layer_norm(DEEPNORM_ALPHA * xn_ref[...] + f, lng_ref[...], lnb_ref[...])


def _combine(xn, tw, yb, ln_g, ln_b, tm):
    t = xn.shape[0]
    nblk = t // tm
    ysp = lambda k: pl.BlockSpec((tm, D_MODEL), lambda i, k=k: (k * nblk + i, 0))
    return pl.pallas_call(
        _combine_kernel,
        out_shape=jax.ShapeDtypeStruct((t, D_MODEL), F32),
        grid=(nblk,),
        in_specs=[pl.BlockSpec((tm, D_MODEL), lambda i: (i, 0)),
                  pl.BlockSpec((tm, LANES), lambda i: (i, 0)),
                  ysp(0), ysp(1), ysp(2), ysp(3),
                  pl.BlockSpec((1, D_MODEL), lambda i: (0, 0)),
                  pl.BlockSpec((1, D_MODEL), lambda i: (0, 0))],
        out_specs=pl.BlockSpec((tm, D_MODEL), lambda i: (i, 0)),
        compiler_params=_cparams(("parallel",)),
        name="combine_ln",
    )(xn, tw, yb, yb, yb, yb, ln_g, ln_b)


def kernel(x, mem, w_in, b_gate, diff_lambda, diff_subln_g, rel_bias, w_mem_kv, w_branch, w_out,
           ln1_g, ln1_b, router_w, router_b, w_gate_up, b_gate_up, w_down, b_down, ln2_g, ln2_b):
    b, s, d = x.shape
    t = b * s
    m = mem.shape[1]
    tq = min(ATT_TILE, s)
    tok_tile = min(TOK_TILE, t)

    idx = jnp.arange(tq, dtype=jnp.int32)
    u = (idx[:, None] > idx[None, :]).astype(BF16)
    bias = _bias_tiles(rel_bias, tq)
    mem2 = mem.reshape(b * m, d)
    xt = x.reshape(t, d)

    for l in range(DEPTH):
        w_in_l = jnp.concatenate([w_in[l][:, 7 * BRANCH_WIDTH:], w_in[l][:, :7 * BRANCH_WIDTH]],
                                 axis=1).astype(BF16)
        proj = _matmul(xt, w_in_l, min(1024, t), 512, "in_proj")
        proj3 = proj.reshape(b, s, IN_WIDTH)
        mem_kv = _matmul(mem2, w_mem_kv[l].astype(BF16), min(1024, b * m), 512, "mem_kv")
        mem_kv = mem_kv.reshape(b, m, 2 * BRANCH_WIDTH)

        lam_init = 0.8 - 0.6 * math.exp(-0.3 * l)
        ysb = _sb_attention(proj3, u, tq)
        ydf = _df_attention(proj3, diff_lambda[l].astype(F32),
                            diff_subln_g[l].reshape(1, 2 * DIFF_HEAD_DIM).astype(F32),
                            bias, tq, lam_init)
        ymem = _mem_attention(proj3, mem_kv, min(512, s))

        rw = router_w[l].astype(F32)
        rwh = rw.astype(BF16)
        rwl = (rw - rwh.astype(F32)).astype(BF16)
        xn, tw, ti = _merge(
            xt, proj, ysb.reshape(t, BRANCH_WIDTH), ydf.reshape(t, BRANCH_WIDTH),
            ymem.reshape(t, BRANCH_WIDTH), b_gate[l].reshape(N_BRANCHES, D_MODEL).astype(F32),
            w_branch[l].astype(BF16), w_out[l].astype(BF16),
            ln1_g[l].reshape(1, d).astype(F32), ln1_b[l].reshape(1, d).astype(F32),
            rwh, rwl, router_b[l].reshape(1, N_EXPERTS).astype(F32), tok_tile)

        te, nv, tok, dst = _route(ti, t, MOE_TILE)
        yb = _moe(xn, te, nv, tok, dst, w_gate_up[l].astype(BF16),
                  b_gate_up[l].reshape(N_EXPERTS, 1, 2 * D_FF).astype(F32),
                  w_down[l].astype(BF16), b_down[l].reshape(N_EXPERTS, 1, D_MODEL).astype(F32),
                  MOE_TILE)
        xt = _combine(xn, tw, yb, ln2_g[l].reshape(1, d).astype(F32),
                      ln2_b[l].reshape(1, d).astype(F32), tok_tile)
    return xt.reshape(b, s, d)
```

```python
import functools
import math

import jax
import jax.numpy as jnp
from jax import lax
from jax.experimental import pallas as pl
from jax.experimental.pallas import tpu as pltpu

F32 = jnp.float32
BF16 = jnp.bfloat16

D_MODEL = 1024
DEPTH = 2
CHUNK = 64
BRANCH_WIDTH = D_MODEL // 2
SB_HEAD_DIM = 64
DIFF_HEAD_DIM = 64
DIFF_HEADS = BRANCH_WIDTH // (2 * DIFF_HEAD_DIM)
MEM_HEAD_DIM = 128
MEM_HEADS = BRANCH_WIDTH // MEM_HEAD_DIM
N_BRANCHES = 3
GATE_WIDTH = N_BRANCHES * D_MODEL
IN_WIDTH = 7 * BRANCH_WIDTH + GATE_WIDTH
NUM_BUCKETS = 32
MAX_DISTANCE = 128
N_EXPERTS = 32
TOP_K = 4
D_FF = D_MODEL
SWIGLU_LIMIT = 7.0
SWIGLU_ALPHA = 1.702
LN_EPS = 1e-5
RMS_EPS = 1e-5
DEEPNORM_ALPHA = (2 * DEPTH) ** 0.25

LANES = 128
SUBLANES = 8
NEG_BIG = -1e30

_GATE_BLK = 0
_SBQ_BLK = GATE_WIDTH // LANES
_SBK_BLK = _SBQ_BLK + 4
_SBV_BLK = _SBQ_BLK + 8
_DFQ_BLK = _SBQ_BLK + 12
_DFK_BLK = _SBQ_BLK + 16
_DFV_BLK = _SBQ_BLK + 20
_MEMQ_BLK = _SBQ_BLK + 24

VMEM_LIMIT = 56 * 1024 * 1024

ATT_TILE = 256
MOE_TILE = 256
TOK_TILE = 512


def _cparams(sem):
    return pltpu.CompilerParams(dimension_semantics=sem, vmem_limit_bytes=VMEM_LIMIT)


def _matmul_kernel(a_ref, w_ref, o_ref):
    a = a_ref[...].astype(BF16)
    o_ref[...] = jnp.dot(a, w_ref[...], preferred_element_type=F32).astype(o_ref.dtype)


def _matmul(a, w, tm, tn, name):
    m, k = a.shape
    n = w.shape[1]
    return pl.pallas_call(
        _matmul_kernel,
        out_shape=jax.ShapeDtypeStruct((m, n), BF16),
        grid=(m // tm, n // tn),
        in_specs=[pl.BlockSpec((tm, k), lambda i, j: (i, 0)),
                  pl.BlockSpec((k, tn), lambda i, j: (0, j))],
        out_specs=pl.BlockSpec((tm, tn), lambda i, j: (i, j)),
        compiler_params=_cparams(("parallel", "arbitrary")),
        name=name,
    )(a, w)


_NT = (((1,), (1,)), ((), ()))


def _sb_kernel(q_ref, k_ref, vt_ref, ut_ref, o_ref, c_ref, acc_ref, *, tq):
    i = pl.program_id(1)
    n_heads = 2 * (BRANCH_WIDTH // LANES)
    low = lax.broadcasted_iota(jnp.int32, (tq, LANES), 1) < SB_HEAD_DIM
    q_heads = []
    for hp in range(n_heads // 2):
        qs = q_ref[0, :, hp * LANES:(hp + 1) * LANES].astype(F32) * (SB_HEAD_DIM ** -0.5)
        q_heads += [jnp.where(low, qs, 0.0).astype(BF16), jnp.where(low, 0.0, qs).astype(BF16)]
    ut = ut_ref[...]
    key = lax.broadcasted_iota(jnp.int32, (tq, tq), 0)
    qry = lax.broadcasted_iota(jnp.int32, (tq, tq), 1)
    earlier = key < qry

    c_ref[...] = jnp.zeros_like(c_ref)
    acc_ref[...] = jnp.zeros_like(acc_ref)

    def step(j, masked):
        start = pl.multiple_of(j * tq, tq)
        zs = []
        for h in range(n_heads):
            kb = k_ref[0, pl.ds(start, tq), (h // 2) * LANES:(h // 2 + 1) * LANES]
            zs.append(lax.dot_general(kb, q_heads[h], _NT, preferred_element_type=F32))
        lss, l1ms = [], []
        for h in range(n_heads):
            z = zs[h]
            lp = jnp.log(1.0 + jnp.exp(-jnp.abs(z)))
            ls = jnp.minimum(z, 0.0) - lp
            l1m = ls - z
            if masked:
                l1m = jnp.where(earlier, l1m, 0.0)
            lss.append(ls)
            l1ms.append(l1m)
        betweens = [jnp.dot(ut, l1ms[h].astype(BF16), preferred_element_type=F32)
                    for h in range(n_heads)]
        a_s = []
        for h in range(n_heads):
            c = c_ref[h:h + 1, :]
            a = jnp.exp(lss[h] + betweens[h] + c)
            if masked:
                a = jnp.where(earlier, a, 0.0)
            a_s.append(a.astype(BF16))
            c_ref[h:h + 1, :] = c + betweens[h][0:1, :] + l1ms[h][0:1, :]
        for h in range(n_heads):
            vtb = vt_ref[0, j, (h // 2) * LANES:(h // 2 + 1) * LANES, :]
            acc_ref[h] += jnp.dot(vtb, a_s[h], preferred_element_type=F32)

    step(i, True)

    def body(n, carry):
        step(i - 1 - n, False)
        return carry

    lax.fori_loop(0, i, body, 0)
    sub = lax.broadcasted_iota(jnp.int32, (LANES, tq), 0)
    for hp in range(n_heads // 2):
        ot = jnp.where(sub < SB_HEAD_DIM, acc_ref[2 * hp], acc_ref[2 * hp + 1])
        o_ref[0, :, hp * LANES:(hp + 1) * LANES] = ot.T.astype(o_ref.dtype)


def _sb_attention(proj, vt, ut, tq):
    b, s, _ = proj.shape
    n_heads = 2 * (BRANCH_WIDTH // LANES)
    wblk = BRANCH_WIDTH // LANES
    kern = functools.partial(_sb_kernel, tq=tq)
    return pl.pallas_call(
        kern,
        out_shape=jax.ShapeDtypeStruct((b, s, BRANCH_WIDTH), BF16),
        grid=(b, s // tq),
        in_specs=[pl.BlockSpec((1, tq, BRANCH_WIDTH), lambda bi, i: (bi, i, _SBQ_BLK // wblk)),
                  pl.BlockSpec((1, s, BRANCH_WIDTH), lambda bi, i: (bi, 0, _SBK_BLK // wblk)),
                  pl.BlockSpec((1, s // tq, BRANCH_WIDTH, tq), lambda bi, i: (bi, 0, 0, 0)),
                  pl.BlockSpec((tq, tq), lambda bi, i: (0, 0))],
        out_specs=pl.BlockSpec((1, tq, BRANCH_WIDTH), lambda bi, i: (bi, i, 0)),
        scratch_shapes=[pltpu.VMEM((n_heads, tq), F32),
                        pltpu.VMEM((n_heads, LANES, tq), F32)],
        compiler_params=_cparams(("parallel", "arbitrary")),
        name="sb_attention",
    )(proj, proj, vt, ut)


def _df_kernel(lam_ref, g_ref, q_ref, k_ref, vt_ref, bias_ref, o_ref, mx_ref, den_ref, acc_ref,
               *, tq, lam_init):
    i = pl.program_id(1)
    n_maps = 2 * DIFF_HEADS
    low = lax.broadcasted_iota(jnp.int32, (tq, LANES), 1) < DIFF_HEAD_DIM
    q_maps = []
    for h in range(DIFF_HEADS):
        qs = q_ref[0, :, h * LANES:(h + 1) * LANES].astype(F32) * (DIFF_HEAD_DIM ** -0.5)
        q_maps += [jnp.where(low, qs, 0.0).astype(BF16), jnp.where(low, 0.0, qs).astype(BF16)]

    mx_ref[...] = jnp.full_like(mx_ref, NEG_BIG)
    den_ref[...] = jnp.zeros_like(den_ref)
    acc_ref[...] = jnp.zeros_like(acc_ref)

    def step(j, kind, off=None):
        start = pl.multiple_of(j * tq, tq)
        scs = []
        for m in range(n_maps):
            kb = k_ref[0, pl.ds(start, tq), (m // 2) * LANES:(m // 2 + 1) * LANES]
            scs.append(lax.dot_general(kb, q_maps[m], _NT, preferred_element_type=F32))
        ps, alphas = [], []
        for m in range(n_maps):
            sc = scs[m]
            if kind is not None:
                sc = sc + bias_ref[m, kind]
            if off is not None:
                sc = sc + off
            mx = mx_ref[m:m + 1, :]
            mx_new = jnp.maximum(mx, jnp.max(sc, axis=0, keepdims=True))
            alpha = jnp.exp(mx - mx_new)
            p = jnp.exp(sc - mx_new)
            den_ref[m:m + 1, :] = alpha * den_ref[m:m + 1, :] + jnp.sum(p, axis=0, keepdims=True)
            mx_ref[m:m + 1, :] = mx_new
            ps.append(p.astype(BF16))
            alphas.append(alpha)
        for m in range(n_maps):
            vtb = vt_ref[0, j, (m // 2) * LANES:(m // 2 + 1) * LANES, :]
            acc_ref[m] = alphas[m] * acc_ref[m] + jnp.dot(vtb, ps[m], preferred_element_type=F32)

    step(i, 0)
    step(jnp.maximum(i - 1, 0), 1, jnp.where(i == 0, NEG_BIG, 0.0).astype(F32))

    def body(n, carry):
        step(i - 2 - n, None)
        return carry

    lax.fori_loop(0, jnp.maximum(i - 1, 0), body, 0)

    lf = lam_ref[...]
    lam = (jnp.exp(jnp.sum(lf[0:1] * lf[1:2], axis=-1, keepdims=True))
           - jnp.exp(jnp.sum(lf[2:3] * lf[3:4], axis=-1, keepdims=True)) + lam_init)
    for h in range(DIFF_HEADS):
        m0, m1 = 2 * h, 2 * h + 1
        o = (acc_ref[m0] / den_ref[m0:m0 + 1, :]
             - lam * (acc_ref[m1] / den_ref[m1:m1 + 1, :]))
        o = o * lax.rsqrt(jnp.mean(o * o, axis=0, keepdims=True) + RMS_EPS)
        o_ref[0, :, h * LANES:(h + 1) * LANES] = (
            o.T * g_ref[...] * (1.0 - lam_init)).astype(o_ref.dtype)


def _df_attention(proj, vt, lam_rows, subln_g, bias, tq, lam_init):
    b, s, _ = proj.shape
    n_maps = 2 * DIFF_HEADS
    wblk = BRANCH_WIDTH // LANES
    kern = functools.partial(_df_kernel, tq=tq, lam_init=lam_init)
    return pl.pallas_call(
        kern,
        out_shape=jax.ShapeDtypeStruct((b, s, BRANCH_WIDTH), BF16),
        grid=(b, s // tq),
        in_specs=[pl.BlockSpec((4, DIFF_HEAD_DIM), lambda bi, i: (0, 0)),
                  pl.BlockSpec((1, LANES), lambda bi, i: (0, 0)),
                  pl.BlockSpec((1, tq, BRANCH_WIDTH), lambda bi, i: (bi, i, _DFQ_BLK // wblk)),
                  pl.BlockSpec((1, s, BRANCH_WIDTH), lambda bi, i: (bi, 0, _DFK_BLK // wblk)),
                  pl.BlockSpec((1, s // tq, BRANCH_WIDTH, tq), lambda bi, i: (bi, 0, 0, 0)),
                  pl.BlockSpec((n_maps, 2, tq, tq), lambda bi, i: (0, 0, 0, 0))],
        out_specs=pl.BlockSpec((1, tq, BRANCH_WIDTH), lambda bi, i: (bi, i, 0)),
        scratch_shapes=[pltpu.VMEM((n_maps, tq), F32),
                        pltpu.VMEM((n_maps, tq), F32),
                        pltpu.VMEM((n_maps, LANES, tq), F32)],
        compiler_params=_cparams(("parallel", "arbitrary")),
        name="diff_attention",
    )(lam_rows, subln_g, proj, proj, vt, bias)


def _key_major(v, tk):
    b, s, w = v.shape
    return v.reshape(b, s // tk, tk, w).swapaxes(2, 3)


def _t5_bucket(rel):
    half = NUM_BUCKETS // 2
    max_exact = half // 2
    n = jnp.abs(rel)
    nf = jnp.maximum(n, 1).astype(F32)
    large = max_exact + (jnp.log(nf / max_exact) / math.log(MAX_DISTANCE / max_exact)
                         * (half - max_exact)).astype(jnp.int32)
    large = jnp.minimum(large, half - 1)
    return jnp.where(rel > 0, half, 0) + jnp.where(n < max_exact, n, large)


def _bias_tiles(rel_bias, tq):
    table = rel_bias.astype(F32)
    r = jnp.arange(tq, dtype=jnp.int32)
    rel_diag = r[None, :] - r[:, None]
    far = table[_t5_bucket(jnp.full((), -(tq + 1), jnp.int32))]
    diag = table[_t5_bucket(rel_diag)] - far
    visible = (r[None, :] // CHUNK) <= (r[:, None] // CHUNK)
    diag = jnp.where(visible[..., None], diag, NEG_BIG)
    near = table[_t5_bucket(rel_diag - tq)] - far
    return jnp.stack([diag, near], axis=0).transpose(3, 0, 2, 1)


def _mem_kernel(q_ref, kv_ref, o_ref):
    for h in range(MEM_HEADS):
        lo = h * MEM_HEAD_DIM
        q = (q_ref[0, :, lo:lo + MEM_HEAD_DIM].astype(F32) * (MEM_HEAD_DIM ** -0.5)).astype(BF16)
        k = kv_ref[0, :, lo:lo + MEM_HEAD_DIM]
        v = kv_ref[0, :, BRANCH_WIDTH + lo:BRANCH_WIDTH + lo + MEM_HEAD_DIM]
        sc = lax.dot_general(q, k, (((1,), (1,)), ((), ())), preferred_element_type=F32)
        p = jnp.exp(sc - jnp.max(sc, axis=-1, keepdims=True))
        den = jnp.sum(p, axis=-1, keepdims=True)
        o = jnp.dot(p.astype(BF16), v, preferred_element_type=F32) / den
        o_ref[0, :, lo:lo + MEM_HEAD_DIM] = o.astype(o_ref.dtype)


def _mem_attention(proj, mem_kv, tq):
    b, s, _ = proj.shape
    m = mem_kv.shape[1]
    return pl.pallas_call(
        _mem_kernel,
        out_shape=jax.ShapeDtypeStruct((b, s, BRANCH_WIDTH), BF16),
        grid=(b, s // tq),
        in_specs=[pl.BlockSpec((1, tq, BRANCH_WIDTH),
                               lambda bi, i: (bi, i, _MEMQ_BLK * LANES // BRANCH_WIDTH)),
                  pl.BlockSpec((1, m, 2 * BRANCH_WIDTH), lambda bi, i: (bi, 0, 0))],
        out_specs=pl.BlockSpec((1, tq, BRANCH_WIDTH), lambda bi, i: (bi, i, 0)),
        compiler_params=_cparams(("parallel", "arbitrary")),
        name="mem_attention",
    )(proj, mem_kv)


def _layer_norm(r, g, b):
    mu = jnp.mean(r, axis=-1, keepdims=True)
    rc = r - mu
    var = jnp.mean(rc * rc, axis=-1, keepdims=True)
    return rc * lax.rsqrt(var + LN_EPS) * g + b


def _merge_kernel(x_ref, g0_ref, g1_ref, g2_ref, ysb_ref, ydf_ref, ymem_ref, bg_ref, wb_ref,
                  wo_ref, lng_ref, lnb_ref, rwh_ref, rwl_ref, rb_ref, xn_ref, tw_ref, ti_ref):
    merged = None
    for br, (g_ref, y_ref) in enumerate(((g0_ref, ysb_ref), (g1_ref, ydf_ref), (g2_ref, ymem_ref))):
        gate = jax.nn.sigmoid(g_ref[...].astype(F32) + bg_ref[br:br + 1, :])
        term = gate * jnp.dot(y_ref[...], wb_ref[br], preferred_element_type=F32)
        merged = term if merged is None else merged + term
    h = jnp.dot(merged.astype(BF16), wo_ref[...], preferred_element_type=F32)
    xn = _layer_norm(DEEPNORM_ALPHA * x_ref[...].astype(F32) + h, lng_ref[...], lnb_ref[...])
    xn_ref[...] = xn

    xh = xn.astype(BF16)
    xl = (xn - xh.astype(F32)).astype(BF16)
    logits = (jnp.dot(xh, rwh_ref[...], preferred_element_type=F32)
              + jnp.dot(xl, rwh_ref[...], preferred_element_type=F32)
              + jnp.dot(xh, rwl_ref[...], preferred_element_type=F32)) + rb_ref[...]
    tm = logits.shape[0]
    eid = lax.broadcasted_iota(jnp.int32, (tm, N_EXPERTS), 1).astype(F32)
    cur = logits
    vals, idxs = [], []
    for _ in range(TOP_K):
        best = jnp.max(cur, axis=-1, keepdims=True)
        idx = jnp.min(jnp.where(cur == best, eid, float(N_EXPERTS)), axis=-1, keepdims=True)
        vals.append(best)
        idxs.append(idx)
        cur = jnp.where(eid == idx, -jnp.inf, cur)
    exps = [jnp.exp(v - vals[0]) for v in vals]
    den = exps[0] + exps[1] + exps[2] + exps[3]
    lane = lax.broadcasted_iota(jnp.int32, (tm, LANES), 1)
    tw = jnp.zeros((tm, LANES), F32)
    ti = jnp.zeros((tm, LANES), F32)
    for k in range(TOP_K):
        tw = jnp.where(lane == k, exps[k] / den, tw)
        ti = jnp.where(lane == k, idxs[k], ti)
    tw_ref[...] = tw
    ti_ref[...] = ti.astype(jnp.int32)


def _merge(x, proj, ysb, ydf, ymem, b_gate, wb, wo, ln_g, ln_b, rwh, rwl, rb, tm):
    t = x.shape[0]
    tok = lambda blk: pl.BlockSpec((tm, blk[0]), lambda i, c=blk[1]: (i, c))
    full2 = lambda shape: pl.BlockSpec(shape, lambda i: (0, 0))
    return pl.pallas_call(
        _merge_kernel,
        out_shape=(jax.ShapeDtypeStruct((t, D_MODEL), F32),
                   jax.ShapeDtypeStruct((t, LANES), F32),
                   jax.ShapeDtypeStruct((t, LANES), jnp.int32)),
        grid=(t // tm,),
        in_specs=[tok((D_MODEL, 0)),
                  tok((D_MODEL, 0)), tok((D_MODEL, 1)), tok((D_MODEL, 2)),
                  tok((BRANCH_WIDTH, 0)), tok((BRANCH_WIDTH, 0)), tok((BRANCH_WIDTH, 0)),
                  full2((N_BRANCHES, D_MODEL)),
                  pl.BlockSpec((N_BRANCHES, BRANCH_WIDTH, D_MODEL), lambda i: (0, 0, 0)),
                  full2((D_MODEL, D_MODEL)),
                  full2((1, D_MODEL)), full2((1, D_MODEL)),
                  full2((D_MODEL, N_EXPERTS)), full2((D_MODEL, N_EXPERTS)),
                  full2((1, N_EXPERTS))],
        out_specs=(pl.BlockSpec((tm, D_MODEL), lambda i: (i, 0)),
                   pl.BlockSpec((tm, LANES), lambda i: (i, 0)),
                   pl.BlockSpec((tm, LANES), lambda i: (i, 0))),
        compiler_params=_cparams(("parallel",)),
        name="merge_ln_router",
    )(x, proj, proj, proj, ysb, ydf, ymem, b_gate, wb, wo, ln_g, ln_b, rwh, rwl, rb)


def _moe_kernel(te_ref, nv_ref, tok_ref, dst_ref, x_hbm, wgu_ref, bgu_ref, wd_ref, bd_ref,
                y_hbm, xbuf, ybuf, sem_in, sem_out, *, tm):
    g = pl.program_id(0)
    nv = nv_ref[g]

    @pl.when(nv > 0)
    def _():
        def gather(r, carry):
            t = tok_ref[0, 0, r]
            pltpu.make_async_copy(x_hbm.at[pl.ds(t, 1), :], xbuf.at[pl.ds(r, 1), :], sem_in).start()
            return carry

        lax.fori_loop(0, tm, gather, 0)
        pltpu.make_async_copy(x_hbm.at[pl.ds(0, tm), :], xbuf, sem_in).wait()

        xb = xbuf[...].astype(BF16)
        h = jnp.dot(xb, wgu_ref[0], preferred_element_type=F32) + bgu_ref[0]
        gate = jnp.minimum(h[:, :D_FF], SWIGLU_LIMIT)
        up = jnp.clip(h[:, D_FF:], -SWIGLU_LIMIT, SWIGLU_LIMIT)
        act = (up + 1.0) * (gate * jax.nn.sigmoid(SWIGLU_ALPHA * gate))
        ybuf[...] = jnp.dot(act.astype(BF16), wd_ref[0], preferred_element_type=F32) + bd_ref[0]

        def scatter(r, carry):
            d = dst_ref[0, 0, r]
            pltpu.make_async_copy(ybuf.at[pl.ds(r, 1), :], y_hbm.at[pl.ds(d, 1), :], sem_out).start()
            return carry

        lax.fori_loop(0, nv, scatter, 0)

        def wait_rows(n):
            def body(_, carry):
                pltpu.make_async_copy(ybuf.at[pl.ds(0, n), :], y_hbm.at[pl.ds(0, n), :], sem_out).wait()
                return carry
            return body

        lax.fori_loop(0, lax.shift_right_logical(nv, 3), wait_rows(SUBLANES), 0)
        lax.fori_loop(0, nv & (SUBLANES - 1), wait_rows(1), 0)


def _moe(xn, te, nv, tok, dst, wgu, bgu, wd, bd, tm):
    t = xn.shape[0]
    n_tiles = tok.shape[0]
    kern = functools.partial(_moe_kernel, tm=tm)
    grid_spec = pltpu.PrefetchScalarGridSpec(
        num_scalar_prefetch=2,
        grid=(n_tiles,),
        in_specs=[pl.BlockSpec((1, 1, tm), lambda g, te, nv: (g, 0, 0), memory_space=pltpu.SMEM),
                  pl.BlockSpec((1, 1, tm), lambda g, te, nv: (g, 0, 0), memory_space=pltpu.SMEM),
                  pl.BlockSpec(memory_space=pl.ANY),
                  pl.BlockSpec((1, D_MODEL, 2 * D_FF), lambda g, te, nv: (te[g], 0, 0)),
                  pl.BlockSpec((1, 1, 2 * D_FF), lambda g, te, nv: (te[g], 0, 0)),
                  pl.BlockSpec((1, D_FF, D_MODEL), lambda g, te, nv: (te[g], 0, 0)),
                  pl.BlockSpec((1, 1, D_MODEL), lambda g, te, nv: (te[g], 0, 0))],
        out_specs=pl.BlockSpec(memory_space=pl.ANY),
        scratch_shapes=[pltpu.VMEM((tm, D_MODEL), F32),
                        pltpu.VMEM((tm, D_MODEL), F32),
                        pltpu.SemaphoreType.DMA,
                        pltpu.SemaphoreType.DMA],
    )
    return pl.pallas_call(
        kern,
        out_shape=jax.ShapeDtypeStruct((TOP_K * t, D_MODEL), F32),
        grid_spec=grid_spec,
        compiler_params=_cparams(("arbitrary",)),
        name="moe_experts",
    )(te, nv, tok, dst, xn, wgu, bgu, wd, bd)


def _route(topi, t, tm):
    e_flat = topi[:, :TOP_K].T.reshape(-1)
    order = jnp.argsort(e_flat, stable=True).astype(jnp.int32)
    experts = jnp.arange(N_EXPERTS, dtype=jnp.int32)
    counts = jnp.sum((e_flat[:, None] == experts[None, :]).astype(jnp.int32), axis=0)
    off = jnp.cumsum(counts) - counts
    tiles = (counts + tm - 1) // tm
    tile_end = jnp.cumsum(tiles)
    tile_off = tile_end - tiles
    total = tile_end[-1]
    n_tiles = (TOP_K * t) // tm + N_EXPERTS
    g = jnp.arange(n_tiles, dtype=jnp.int32)
    te = jnp.minimum(jnp.searchsorted(tile_end, g, side="right"), N_EXPERTS - 1).astype(jnp.int32)
    r = jnp.arange(tm, dtype=jnp.int32)
    start = (g - tile_off[te]) * tm
    nv = jnp.where(g < total, jnp.clip(counts[te] - start, 0, tm), 0).astype(jnp.int32)
    local = start[:, None] + r[None, :]
    valid = r[None, :] < nv[:, None]
    src = jnp.clip(off[te][:, None] + local, 0, TOP_K * t - 1)
    f = order[src]
    tok = jnp.where(valid, f % t, 0).astype(jnp.int32)
    dst = jnp.where(valid, f, 0).astype(jnp.int32)
    return te, nv, tok.reshape(n_tiles, 1, tm), dst.reshape(n_tiles, 1, tm)


def _combine_kernel(xn_ref, tw_ref, y0_ref, y1_ref, y2_ref, y3_ref, lng_ref, lnb_ref, o_ref):
    tw = tw_ref[...]
    f = None
    for k, y_ref in enumerate((y0_ref, y1_ref, y2_ref, y3_ref)):
        term = tw[:, k:k + 1] * y_ref[...]
        f = term if f is None else f + term
    o_ref[...] = _layer_norm(DEEPNORM_ALPHA * xn_ref[...] + f, lng_ref[...], lnb_ref[...])


def _combine(xn, tw, yb, ln_g, ln_b, tm):
    t = xn.shape[0]
    nblk = t // tm
    ysp = lambda k: pl.BlockSpec((tm, D_MODEL), lambda i, k=k: (k * nblk + i, 0))
    return pl.pallas_call(
        _combine_kernel,
        out_shape=jax.ShapeDtypeStruct((t, D_MODEL), F32),
        grid=(nblk,),
        in_specs=[pl.BlockSpec((tm, D_MODEL), lambda i: (i, 0)),
                  pl.BlockSpec((tm, LANES), lambda i: (i, 0)),
                  ysp(0), ysp(1), ysp(2), ysp(3),
                  pl.BlockSpec((1, D_MODEL), lambda i: (0, 0)),
                  pl.BlockSpec((1, D_MODEL), lambda i: (0, 0))],
        out_specs=pl.BlockSpec((tm, D_MODEL), lambda i: (i, 0)),
        compiler_params=_cparams(("parallel",)),
        name="combine_ln",
    )(xn, tw, yb, yb, yb, yb, ln_g, ln_b)


def kernel(x, mem, w_in, b_gate, diff_lambda, diff_subln_g, rel_bias, w_mem_kv, w_branch, w_out,
           ln1_g, ln1_b, router_w, router_b, w_gate_up, b_gate_up, w_down, b_down, ln2_g, ln2_b):
    b, s, d = x.shape
    t = b * s
    m = mem.shape[1]
    tq = min(ATT_TILE, s)
    tok_tile = min(TOK_TILE, t)

    idx = jnp.arange(tq, dtype=jnp.int32)
    ut = (idx[None, :] > idx[:, None]).astype(BF16)
    bias = _bias_tiles(rel_bias, tq)
    mem2 = mem.reshape(b * m, d)
    xt = x.reshape(t, d)

    for l in range(DEPTH):
        w_in_l = jnp.concatenate([w_in[l][:, 7 * BRANCH_WIDTH:], w_in[l][:, :7 * BRANCH_WIDTH]],
                                 axis=1).astype(BF16)
        proj = _matmul(xt, w_in_l, min(1024, t), 512, "in_proj")
        proj3 = proj.reshape(b, s, IN_WIDTH)
        mem_kv = _matmul(mem2, w_mem_kv[l].astype(BF16), min(1024, b * m), 512, "mem_kv")
        mem_kv = mem_kv.reshape(b, m, 2 * BRANCH_WIDTH)

        lam_init = 0.8 - 0.6 * math.exp(-0.3 * l)
        sbv_t = _key_major(proj3[:, :, _SBV_BLK * LANES:_SBV_BLK * LANES + BRANCH_WIDTH], tq)
        dfv_t = _key_major(proj3[:, :, _DFV_BLK * LANES:_DFV_BLK * LANES + BRANCH_WIDTH], tq)
        ysb = _sb_attention(proj3, sbv_t, ut, tq)
        ydf = _df_attention(proj3, dfv_t, diff_lambda[l].astype(F32),
                            diff_subln_g[l].reshape(1, 2 * DIFF_HEAD_DIM).astype(F32),
                            bias, tq, lam_init)
        ymem = _mem_attention(proj3, mem_kv, min(512, s))

        rw = router_w[l].astype(F32)
        rwh = rw.astype(BF16)
        rwl = (rw - rwh.astype(F32)).astype(BF16)
        xn, tw, ti = _merge(
            xt, proj, ysb.reshape(t, BRANCH_WIDTH), ydf.reshape(t, BRANCH_WIDTH),
            ymem.reshape(t, BRANCH_WIDTH), b_gate[l].reshape(N_BRANCHES, D_MODEL).astype(F32),
            w_branch[l].astype(BF16), w_out[l].astype(BF16),
            ln1_g[l].reshape(1, d).astype(F32), ln1_b[l].reshape(1, d).astype(F32),
            rwh, rwl, router_b[l].reshape(1, N_EXPERTS).astype(F32), tok_tile)

        te, nv, tok, dst = _route(ti, t, MOE_TILE)
        yb = _moe(xn, te, nv, tok, dst, w_gate_up[l].astype(BF16),
                  b_gate_up[l].reshape(N_EXPERTS, 1, 2 * D_FF).astype(F32),
                  w_down[l].astype(BF16), b_down[l].reshape(N_EXPERTS, 1, D_MODEL).astype(F32),
                  MOE_TILE)
        xt = _combine(xn, tw, yb, ln2_g[l].reshape(1, d).astype(F32),
                      ln2_b[l].reshape(1, d).astype(F32), tok_tile)
    return xt.reshape(b, s, d)
```

```python
import functools
import math

import jax
import jax.numpy as jnp
from jax import lax
from jax.experimental import pallas as pl
from jax.experimental.pallas import tpu as pltpu

F32 = jnp.float32
BF16 = jnp.bfloat16

D_MODEL = 1024
DEPTH = 2
CHUNK = 64
BRANCH_WIDTH = D_MODEL // 2
SB_HEAD_DIM = 64
DIFF_HEAD_DIM = 64
DIFF_HEADS = BRANCH_WIDTH // (2 * DIFF_HEAD_DIM)
MEM_HEAD_DIM = 128
MEM_HEADS = BRANCH_WIDTH // MEM_HEAD_DIM
N_BRANCHES = 3
GATE_WIDTH = N_BRANCHES * D_MODEL
IN_WIDTH = 7 * BRANCH_WIDTH + GATE_WIDTH
NUM_BUCKETS = 32
MAX_DISTANCE = 128
N_EXPERTS = 32
TOP_K = 4
D_FF = D_MODEL
SWIGLU_LIMIT = 7.0
SWIGLU_ALPHA = 1.702
LN_EPS = 1e-5
RMS_EPS = 1e-5
DEEPNORM_ALPHA = (2 * DEPTH) ** 0.25
LOG2E = math.log2(math.e)

LANES = 128
SUBLANES = 8
NEG_BIG = -1e30

_GATE_BLK = 0
_SBQ_BLK = GATE_WIDTH // LANES
_SBK_BLK = _SBQ_BLK + 4
_SBV_BLK = _SBQ_BLK + 8
_DFQ_BLK = _SBQ_BLK + 12
_DFK_BLK = _SBQ_BLK + 16
_DFV_BLK = _SBQ_BLK + 20
_MEMQ_BLK = _SBQ_BLK + 24

VMEM_LIMIT = 56 * 1024 * 1024

ATT_TILE = 256
MOE_TILE = 256
TOK_TILE = 512


def _cparams(sem):
    return pltpu.CompilerParams(dimension_semantics=sem, vmem_limit_bytes=VMEM_LIMIT)


def _matmul_kernel(a_ref, w_ref, o_ref):
    a = a_ref[...].astype(BF16)
    o_ref[...] = jnp.dot(a, w_ref[...], preferred_element_type=F32).astype(o_ref.dtype)


def _matmul(a, w, tm, tn, name):
    m, k = a.shape
    n = w.shape[1]
    return pl.pallas_call(
        _matmul_kernel,
        out_shape=jax.ShapeDtypeStruct((m, n), BF16),
        grid=(m // tm, n // tn),
        in_specs=[pl.BlockSpec((tm, k), lambda i, j: (i, 0)),
                  pl.BlockSpec((k, tn), lambda i, j: (0, j))],
        out_specs=pl.BlockSpec((tm, tn), lambda i, j: (i, j)),
        compiler_params=_cparams(("parallel", "arbitrary")),
        name=name,
    )(a, w)


_NT = (((1,), (1,)), ((), ()))


def _sb_kernel(q_ref, k_ref, vt_ref, ut_ref, o_ref, c_ref, acc_ref, *, tq):
    i = pl.program_id(1)
    n_heads = 2 * (BRANCH_WIDTH // LANES)
    low = lax.broadcasted_iota(jnp.int32, (tq, LANES), 1) < SB_HEAD_DIM
    q_heads = []
    for hp in range(n_heads // 2):
        qs = q_ref[0, :, hp * LANES:(hp + 1) * LANES].astype(F32) * (-LOG2E * SB_HEAD_DIM ** -0.5)
        q_heads += [jnp.where(low, qs, 0.0).astype(BF16), jnp.where(low, 0.0, qs).astype(BF16)]
    ut = ut_ref[...]
    key = lax.broadcasted_iota(jnp.int32, (tq, tq), 0)
    qry = lax.broadcasted_iota(jnp.int32, (tq, tq), 1)
    earlier = key < qry

    c_ref[...] = jnp.zeros_like(c_ref)
    acc_ref[...] = jnp.zeros_like(acc_ref)

    def step(j, masked):
        start = pl.multiple_of(j * tq, tq)
        zns = []
        for h in range(n_heads):
            kb = k_ref[0, pl.ds(start, tq), (h // 2) * LANES:(h // 2 + 1) * LANES]
            zns.append(lax.dot_general(kb, q_heads[h], _NT, preferred_element_type=F32))
        l1ms = []
        for h in range(n_heads):
            zn = zns[h]
            neg_abs = lax.bitcast_convert_type(
                lax.bitcast_convert_type(zn, jnp.uint32) | jnp.uint32(0x80000000), F32)
            l1m = jnp.minimum(zn, 0.0) - jnp.log2(1.0 + jnp.exp2(neg_abs))
            if masked:
                l1m = jnp.where(earlier, l1m, 0.0)
            l1ms.append(l1m.astype(BF16))
        incls = [jnp.dot(ut, l1ms[h], preferred_element_type=F32) for h in range(n_heads)]
        a_s = []
        for h in range(n_heads):
            c = c_ref[h:h + 1, :]
            a = jnp.exp2(incls[h] + c - zns[h])
            if masked:
                a = jnp.where(earlier, a, 0.0)
            a_s.append(a.astype(BF16))
            c_ref[h:h + 1, :] = c + incls[h][0:1, :]
        for h in range(n_heads):
            vtb = vt_ref[0, j, (h // 2) * LANES:(h // 2 + 1) * LANES, :]
            acc_ref[h] += jnp.dot(vtb, a_s[h], preferred_element_type=F32)

    step(i, True)

    def body(n, carry):
        step(i - 1 - n, False)
        return carry

    lax.fori_loop(0, i, body, 0)
    sub = lax.broadcasted_iota(jnp.int32, (LANES, tq), 0)
    for hp in range(n_heads // 2):
        ot = jnp.where(sub < SB_HEAD_DIM, acc_ref[2 * hp], acc_ref[2 * hp + 1])
        o_ref[0, :, hp * LANES:(hp + 1) * LANES] = ot.T.astype(o_ref.dtype)


def _sb_attention(proj, vt, ut, tq):
    b, s, _ = proj.shape
    n_heads = 2 * (BRANCH_WIDTH // LANES)
    wblk = BRANCH_WIDTH // LANES
    kern = functools.partial(_sb_kernel, tq=tq)
    return pl.pallas_call(
        kern,
        out_shape=jax.ShapeDtypeStruct((b, s, BRANCH_WIDTH), BF16),
        grid=(b, s // tq),
        in_specs=[pl.BlockSpec((1, tq, BRANCH_WIDTH), lambda bi, i: (bi, i, _SBQ_BLK // wblk)),
                  pl.BlockSpec((1, s, BRANCH_WIDTH), lambda bi, i: (bi, 0, _SBK_BLK // wblk)),
                  pl.BlockSpec((1, s // tq, BRANCH_WIDTH, tq), lambda bi, i: (bi, 0, 0, 0)),
                  pl.BlockSpec((tq, tq), lambda bi, i: (0, 0))],
        out_specs=pl.BlockSpec((1, tq, BRANCH_WIDTH), lambda bi, i: (bi, i, 0)),
        scratch_shapes=[pltpu.VMEM((n_heads, tq), F32),
                        pltpu.VMEM((n_heads, LANES, tq), F32)],
        compiler_params=_cparams(("parallel", "arbitrary")),
        name="sb_attention",
    )(proj, proj, vt, ut)


def _df_kernel(lam_ref, g_ref, q_ref, k_ref, vt_ref, bias_ref, o_ref, mx_ref, den_ref, acc_ref,
               *, tq, lam_init):
    i = pl.program_id(1)
    n_maps = 2 * DIFF_HEADS
    low = lax.broadcasted_iota(jnp.int32, (tq, LANES), 1) < DIFF_HEAD_DIM
    q_maps = []
    for h in range(DIFF_HEADS):
        qs = q_ref[0, :, h * LANES:(h + 1) * LANES].astype(F32) * (LOG2E * DIFF_HEAD_DIM ** -0.5)
        q_maps += [jnp.where(low, qs, 0.0).astype(BF16), jnp.where(low, 0.0, qs).astype(BF16)]

    mx_ref[...] = jnp.full_like(mx_ref, NEG_BIG)
    den_ref[...] = jnp.zeros_like(den_ref)
    acc_ref[...] = jnp.zeros_like(acc_ref)

    def step(j, kind, off=None):
        start = pl.multiple_of(j * tq, tq)
        scs = []
        for m in range(n_maps):
            kb = k_ref[0, pl.ds(start, tq), (m // 2) * LANES:(m // 2 + 1) * LANES]
            scs.append(lax.dot_general(kb, q_maps[m], _NT, preferred_element_type=F32))
        ps, alphas = [], []
        for m in range(n_maps):
            sc = scs[m]
            if kind is not None:
                sc = sc + bias_ref[m, kind]
            if off is not None:
                sc = sc + off
            mx = mx_ref[m:m + 1, :]
            mx_new = jnp.maximum(mx, jnp.max(sc, axis=0, keepdims=True))
            alpha = jnp.exp2(mx - mx_new)
            p = jnp.exp2(sc - mx_new)
            den_ref[m:m + 1, :] = alpha * den_ref[m:m + 1, :] + jnp.sum(p, axis=0, keepdims=True)
            mx_ref[m:m + 1, :] = mx_new
            ps.append(p.astype(BF16))
            alphas.append(alpha)
        for m in range(n_maps):
            vtb = vt_ref[0, j, (m // 2) * LANES:(m // 2 + 1) * LANES, :]
            acc_ref[m] = alphas[m] * acc_ref[m] + jnp.dot(vtb, ps[m], preferred_element_type=F32)

    step(i, 0)
    step(jnp.maximum(i - 1, 0), 1, jnp.where(i == 0, NEG_BIG, 0.0).astype(F32))

    def body(n, carry):
        step(i - 2 - n, None)
        return carry

    lax.fori_loop(0, jnp.maximum(i - 1, 0), body, 0)

    lf = lam_ref[...]
    lam = (jnp.exp(jnp.sum(lf[0:1] * lf[1:2], axis=-1, keepdims=True))
           - jnp.exp(jnp.sum(lf[2:3] * lf[3:4], axis=-1, keepdims=True)) + lam_init)
    for h in range(DIFF_HEADS):
        m0, m1 = 2 * h, 2 * h + 1
        o = (acc_ref[m0] / den_ref[m0:m0 + 1, :]
             - lam * (acc_ref[m1] / den_ref[m1:m1 + 1, :]))
        o = o * lax.rsqrt(jnp.mean(o * o, axis=0, keepdims=True) + RMS_EPS)
        o_ref[0, :, h * LANES:(h + 1) * LANES] = (
            o.T * g_ref[...] * (1.0 - lam_init)).astype(o_ref.dtype)


def _df_attention(proj, vt, lam_rows, subln_g, bias, tq, lam_init):
    b, s, _ = proj.shape
    n_maps = 2 * DIFF_HEADS
    wblk = BRANCH_WIDTH // LANES
    kern = functools.partial(_df_kernel, tq=tq, lam_init=lam_init)
    return pl.pallas_call(
        kern,
        out_shape=jax.ShapeDtypeStruct((b, s, BRANCH_WIDTH), BF16),
        grid=(b, s // tq),
        in_specs=[pl.BlockSpec((4, DIFF_HEAD_DIM), lambda bi, i: (0, 0)),
                  pl.BlockSpec((1, LANES), lambda bi, i: (0, 0)),
                  pl.BlockSpec((1, tq, BRANCH_WIDTH), lambda bi, i: (bi, i, _DFQ_BLK // wblk)),
                  pl.BlockSpec((1, s, BRANCH_WIDTH), lambda bi, i: (bi, 0, _DFK_BLK // wblk)),
                  pl.BlockSpec((1, s // tq, BRANCH_WIDTH, tq), lambda bi, i: (bi, 0, 0, 0)),
                  pl.BlockSpec((n_maps, 2, tq, tq), lambda bi, i: (0, 0, 0, 0))],
        out_specs=pl.BlockSpec((1, tq, BRANCH_WIDTH), lambda bi, i: (bi, i, 0)),
        scratch_shapes=[pltpu.VMEM((n_maps, tq), F32),
                        pltpu.VMEM((n_maps, tq), F32),
                        pltpu.VMEM((n_maps, LANES, tq), F32)],
        compiler_params=_cparams(("parallel", "arbitrary")),
        name="diff_attention",
    )(lam_rows, subln_g, proj, proj, vt, bias)


def _key_major(v, tk):
    b, s, w = v.shape
    return v.reshape(b, s // tk, tk, w).swapaxes(2, 3)


def _t5_bucket(rel):
    half = NUM_BUCKETS // 2
    max_exact = half // 2
    n = jnp.abs(rel)
    nf = jnp.maximum(n, 1).astype(F32)
    large = max_exact + (jnp.log(nf / max_exact) / math.log(MAX_DISTANCE / max_exact)
                         * (half - max_exact)).astype(jnp.int32)
    large = jnp.minimum(large, half - 1)
    return jnp.where(rel > 0, half, 0) + jnp.where(n < max_exact, n, large)


def _bias_tiles(rel_bias, tq):
    table = rel_bias.astype(F32)
    r = jnp.arange(tq, dtype=jnp.int32)
    rel_diag = r[None, :] - r[:, None]
    far = table[_t5_bucket(jnp.full((), -(tq + 1), jnp.int32))]
    diag = (table[_t5_bucket(rel_diag)] - far) * LOG2E
    visible = (r[None, :] // CHUNK) <= (r[:, None] // CHUNK)
    diag = jnp.where(visible[..., None], diag, NEG_BIG)
    near = (table[_t5_bucket(rel_diag - tq)] - far) * LOG2E
    return jnp.stack([diag, near], axis=0).transpose(3, 0, 2, 1)


def _mem_kernel(q_ref, kv_ref, o_ref):
    for h in range(MEM_HEADS):
        lo = h * MEM_HEAD_DIM
        q = (q_ref[0, :, lo:lo + MEM_HEAD_DIM].astype(F32) * (MEM_HEAD_DIM ** -0.5)).astype(BF16)
        k = kv_ref[0, :, lo:lo + MEM_HEAD_DIM]
        v = kv_ref[0, :, BRANCH_WIDTH + lo:BRANCH_WIDTH + lo + MEM_HEAD_DIM]
        sc = lax.dot_general(q, k, (((1,), (1,)), ((), ())), preferred_element_type=F32)
        p = jnp.exp(sc - jnp.max(sc, axis=-1, keepdims=True))
        den = jnp.sum(p, axis=-1, keepdims=True)
        o = jnp.dot(p.astype(BF16), v, preferred_element_type=F32) / den
        o_ref[0, :, lo:lo + MEM_HEAD_DIM] = o.astype(o_ref.dtype)


def _mem_attention(proj, mem_kv, tq):
    b, s, _ = proj.shape
    m = mem_kv.shape[1]
    return pl.pallas_call(
        _mem_kernel,
        out_shape=jax.ShapeDtypeStruct((b, s, BRANCH_WIDTH), BF16),
        grid=(b, s // tq),
        in_specs=[pl.BlockSpec((1, tq, BRANCH_WIDTH),
                               lambda bi, i: (bi, i, _MEMQ_BLK * LANES // BRANCH_WIDTH)),
                  pl.BlockSpec((1, m, 2 * BRANCH_WIDTH), lambda bi, i: (bi, 0, 0))],
        out_specs=pl.BlockSpec((1, tq, BRANCH_WIDTH), lambda bi, i: (bi, i, 0)),
        compiler_params=_cparams(("parallel", "arbitrary")),
        name="mem_attention",
    )(proj, mem_kv)


def _layer_norm(r, g, b):
    mu = jnp.mean(r, axis=-1, keepdims=True)
    rc = r - mu
    var = jnp.mean(rc * rc, axis=-1, keepdims=True)
    return rc * lax.rsqrt(var + LN_EPS) * g + b


def _merge_kernel(x_ref, g0_ref, g1_ref, g2_ref, ysb_ref, ydf_ref, ymem_ref, bg_ref, wb_ref,
                  wo_ref, lng_ref, lnb_ref, rwh_ref, rwl_ref, rb_ref, xn_ref, tw_ref, ti_ref):
    merged = None
    for br, (g_ref, y_ref) in enumerate(((g0_ref, ysb_ref), (g1_ref, ydf_ref), (g2_ref, ymem_ref))):
        gate = jax.nn.sigmoid(g_ref[...].astype(F32) + bg_ref[br:br + 1, :])
        term = gate * jnp.dot(y_ref[...], wb_ref[br], preferred_element_type=F32)
        merged = term if merged is None else merged + term
    h = jnp.dot(merged.astype(BF16), wo_ref[...], preferred_element_type=F32)
    xn = _layer_norm(DEEPNORM_ALPHA * x_ref[...].astype(F32) + h, lng_ref[...], lnb_ref[...])
    xn_ref[...] = xn

    xh = xn.astype(BF16)
    xl = (xn - xh.astype(F32)).astype(BF16)
    logits = (jnp.dot(xh, rwh_ref[...], preferred_element_type=F32)
              + jnp.dot(xl, rwh_ref[...], preferred_element_type=F32)
              + jnp.dot(xh, rwl_ref[...], preferred_element_type=F32)) + rb_ref[...]
    tm = logits.shape[0]
    eid = lax.broadcasted_iota(jnp.int32, (tm, N_EXPERTS), 1).astype(F32)
    cur = logits
    vals, idxs = [], []
    for _ in range(TOP_K):
        best = jnp.max(cur, axis=-1, keepdims=True)
        idx = jnp.min(jnp.where(cur == best, eid, float(N_EXPERTS)), axis=-1, keepdims=True)
        vals.append(best)
        idxs.append(idx)
        cur = jnp.where(eid == idx, -jnp.inf, cur)
    exps = [jnp.exp(v - vals[0]) for v in vals]
    den = exps[0] + exps[1] + exps[2] + exps[3]
    lane = lax.broadcasted_iota(jnp.int32, (tm, LANES), 1)
    tw = jnp.zeros((tm, LANES), F32)
    ti = jnp.zeros((tm, LANES), F32)
    for k in range(TOP_K):
        tw = jnp.where(lane == k, exps[k] / den, tw)
        ti = jnp.where(lane == k, idxs[k], ti)
    tw_ref[...] = tw
    ti_ref[...] = ti.astype(jnp.int32)


def _merge(x, proj, ysb, ydf, ymem, b_gate, wb, wo, ln_g, ln_b, rwh, rwl, rb, tm):
    t = x.shape[0]
    tok = lambda blk: pl.BlockSpec((tm, blk[0]), lambda i, c=blk[1]: (i, c))
    full2 = lambda shape: pl.BlockSpec(shape, lambda i: (0, 0))
    return pl.pallas_call(
        _merge_kernel,
        out_shape=(jax.ShapeDtypeStruct((t, D_MODEL), F32),
                   jax.ShapeDtypeStruct((t, LANES), F32),
                   jax.ShapeDtypeStruct((t, LANES), jnp.int32)),
        grid=(t // tm,),
        in_specs=[tok((D_MODEL, 0)),
                  tok((D_MODEL, 0)), tok((D_MODEL, 1)), tok((D_MODEL, 2)),
                  tok((BRANCH_WIDTH, 0)), tok((BRANCH_WIDTH, 0)), tok((BRANCH_WIDTH, 0)),
                  full2((N_BRANCHES, D_MODEL)),
                  pl.BlockSpec((N_BRANCHES, BRANCH_WIDTH, D_MODEL), lambda i: (0, 0, 0)),
                  full2((D_MODEL, D_MODEL)),
                  full2((1, D_MODEL)), full2((1, D_MODEL)),
                  full2((D_MODEL, N_EXPERTS)), full2((D_MODEL, N_EXPERTS)),
                  full2((1, N_EXPERTS))],
        out_specs=(pl.BlockSpec((tm, D_MODEL), lambda i: (i, 0)),
                   pl.BlockSpec((tm, LANES), lambda i: (i, 0)),
                   pl.BlockSpec((tm, LANES), lambda i: (i, 0))),
        compiler_params=_cparams(("parallel",)),
        name="merge_ln_router",
    )(x, proj, proj, proj, ysb, ydf, ymem, b_gate, wb, wo, ln_g, ln_b, rwh, rwl, rb)


def _moe_kernel(te_ref, nv_ref, tok_ref, dst_ref, x_hbm, wgu_ref, bgu_ref, wd_ref, bd_ref,
                y_hbm, xbuf, ybuf, sem_in, sem_out, *, tm):
    g = pl.program_id(0)
    nv = nv_ref[g]

    @pl.when(nv > 0)
    def _():
        def gather(r, carry):
            t = tok_ref[0, 0, r]
            pltpu.make_async_copy(x_hbm.at[pl.ds(t, 1), :], xbuf.at[pl.ds(r, 1), :], sem_in).start()
            return carry

        lax.fori_loop(0, tm, gather, 0)
        pltpu.make_async_copy(x_hbm.at[pl.ds(0, tm), :], xbuf, sem_in).wait()

        xb = xbuf[...].astype(BF16)
        h = jnp.dot(xb, wgu_ref[0], preferred_element_type=F32) + bgu_ref[0]
        gate = jnp.minimum(h[:, :D_FF], SWIGLU_LIMIT)
        up = jnp.clip(h[:, D_FF:], -SWIGLU_LIMIT, SWIGLU_LIMIT)
        act = (up + 1.0) * (gate * jax.nn.sigmoid(SWIGLU_ALPHA * gate))
        ybuf[...] = jnp.dot(act.astype(BF16), wd_ref[0], preferred_element_type=F32) + bd_ref[0]

        def scatter(r, carry):
            d = dst_ref[0, 0, r]
            pltpu.make_async_copy(ybuf.at[pl.ds(r, 1), :], y_hbm.at[pl.ds(d, 1), :], sem_out).start()
            return carry

        lax.fori_loop(0, nv, scatter, 0)

        def wait_rows(n):
            def body(_, carry):
                pltpu.make_async_copy(ybuf.at[pl.ds(0, n), :], y_hbm.at[pl.ds(0, n), :], sem_out).wait()
                return carry
            return body

        lax.fori_loop(0, lax.shift_right_logical(nv, 3), wait_rows(SUBLANES), 0)
        lax.fori_loop(0, nv & (SUBLANES - 1), wait_rows(1), 0)


def _moe(xn, te, nv, tok, dst, wgu, bgu, wd, bd, tm):
    t = xn.shape[0]
    n_tiles = tok.shape[0]
    kern = functools.partial(_moe_kernel, tm=tm)
    grid_spec = pltpu.PrefetchScalarGridSpec(
        num_scalar_prefetch=2,
        grid=(n_tiles,),
        in_specs=[pl.BlockSpec((1, 1, tm), lambda g, te, nv: (g, 0, 0), memory_space=pltpu.SMEM),
                  pl.BlockSpec((1, 1, tm), lambda g, te, nv: (g, 0, 0), memory_space=pltpu.SMEM),
                  pl.BlockSpec(memory_space=pl.ANY),
                  pl.BlockSpec((1, D_MODEL, 2 * D_FF), lambda g, te, nv: (te[g], 0, 0)),
                  pl.BlockSpec((1, 1, 2 * D_FF), lambda g, te, nv: (te[g], 0, 0)),
                  pl.BlockSpec((1, D_FF, D_MODEL), lambda g, te, nv: (te[g], 0, 0)),
                  pl.BlockSpec((1, 1, D_MODEL), lambda g, te, nv: (te[g], 0, 0))],
        out_specs=pl.BlockSpec(memory_space=pl.ANY),
        scratch_shapes=[pltpu.VMEM((tm, D_MODEL), F32),
                        pltpu.VMEM((tm, D_MODEL), F32),
                        pltpu.SemaphoreType.DMA,
                        pltpu.SemaphoreType.DMA],
    )
    return pl.pallas_call(
        kern,
        out_shape=jax.ShapeDtypeStruct((TOP_K * t, D_MODEL), F32),
        grid_spec=grid_spec,
        compiler_params=_cparams(("arbitrary",)),
        name="moe_experts",
    )(te, nv, tok, dst, xn, wgu, bgu, wd, bd)


def _route(topi, t, tm):
    e_flat = topi[:, :TOP_K].T.reshape(-1)
    order = jnp.argsort(e_flat, stable=True).astype(jnp.int32)
    experts = jnp.arange(N_EXPERTS, dtype=jnp.int32)
    counts = jnp.sum((e_flat[:, None] == experts[None, :]).astype(jnp.int32), axis=0)
    off = jnp.cumsum(counts) - counts
    tiles = (counts + tm - 1) // tm
    tile_end = jnp.cumsum(tiles)
    tile_off = tile_end - tiles
    total = tile_end[-1]
    n_tiles = (TOP_K * t) // tm + N_EXPERTS
    g = jnp.arange(n_tiles, dtype=jnp.int32)
    te = jnp.minimum(jnp.searchsorted(tile_end, g, side="right"), N_EXPERTS - 1).astype(jnp.int32)
    r = jnp.arange(tm, dtype=jnp.int32)
    start = (g - tile_off[te]) * tm
    nv = jnp.where(g < total, jnp.clip(counts[te] - start, 0, tm), 0).astype(jnp.int32)
    local = start[:, None] + r[None, :]
    valid = r[None, :] < nv[:, None]
    src = jnp.clip(off[te][:, None] + local, 0, TOP_K * t - 1)
    f = order[src]
    tok = jnp.where(valid, f % t, 0).astype(jnp.int32)
    dst = jnp.where(valid, f, 0).astype(jnp.int32)
    return te, nv, tok.reshape(n_tiles, 1, tm), dst.reshape(n_tiles, 1, tm)


def _combine_kernel(xn_ref, tw_ref, y0_ref, y1_ref, y2_ref, y3_ref, lng_ref, lnb_ref, o_ref):
    tw = tw_ref[...]
    f = None
    for k, y_ref in enumerate((y0_ref, y1_ref, y2_ref, y3_ref)):
        term = tw[:, k:k + 1] * y_ref[...]
        f = term if f is None else f + term
    o_ref[...] = _layer_norm(DEEPNORM_ALPHA * xn_ref[...] + f, lng_ref[...], lnb_ref[...])


def _combine(xn, tw, yb, ln_g, ln_b, tm):
    t = xn.shape[0]
    nblk = t // tm
    ysp = lambda k: pl.BlockSpec((tm, D_MODEL), lambda i, k=k: (k * nblk + i, 0))
    return pl.pallas_call(
        _combine_kernel,
        out_shape=jax.ShapeDtypeStruct((t, D_MODEL), F32),
        grid=(nblk,),
        in_specs=[pl.BlockSpec((tm, D_MODEL), lambda i: (i, 0)),
                  pl.BlockSpec((tm, LANES), lambda i: (i, 0)),
                  ysp(0), ysp(1), ysp(2), ysp(3),
                  pl.BlockSpec((1, D_MODEL), lambda i: (0, 0)),
                  pl.BlockSpec((1, D_MODEL), lambda i: (0, 0))],
        out_specs=pl.BlockSpec((tm, D_MODEL), lambda i: (i, 0)),
        compiler_params=_cparams(("parallel",)),
        name="combine_ln",
    )(xn, tw, yb, yb, yb, yb, ln_g, ln_b)


def kernel(x, mem, w_in, b_gate, diff_lambda, diff_subln_g, rel_bias, w_mem_kv, w_branch, w_out,
           ln1_g, ln1_b, router_w, router_b, w_gate_up, b_gate_up, w_down, b_down, ln2_g, ln2_b):
    b, s, d = x.shape
    t = b * s
    m = mem.shape[1]
    tq = min(ATT_TILE, s)
    tok_tile = min(TOK_TILE, t)

    idx = jnp.arange(tq, dtype=jnp.int32)
    ut = (idx[None, :] >= idx[:, None]).astype(BF16)
    bias = _bias_tiles(rel_bias, tq)
    mem2 = mem.reshape(b * m, d)
    xt = x.reshape(t, d)

    for l in range(DEPTH):
        w_in_l = jnp.concatenate([w_in[l][:, 7 * BRANCH_WIDTH:], w_in[l][:, :7 * BRANCH_WIDTH]],
                                 axis=1).astype(BF16)
        proj = _matmul(xt, w_in_l, min(1024, t), 512, "in_proj")
        proj3 = proj.reshape(b, s, IN_WIDTH)
        mem_kv = _matmul(mem2, w_mem_kv[l].astype(BF16), min(1024, b * m), 512, "mem_kv")
        mem_kv = mem_kv.reshape(b, m, 2 * BRANCH_WIDTH)

        lam_init = 0.8 - 0.6 * math.exp(-0.3 * l)
        sbv_t = _key_major(proj3[:, :, _SBV_BLK * LANES:_SBV_BLK * LANES + BRANCH_WIDTH], tq)
        dfv_t = _key_major(proj3[:, :, _DFV_BLK * LANES:_DFV_BLK * LANES + BRANCH_WIDTH], tq)
        ysb = _sb_attention(proj3, sbv_t, ut, tq)
        ydf = _df_attention(proj3, dfv_t, diff_lambda[l].astype(F32),
                            diff_subln_g[l].reshape(1, 2 * DIFF_HEAD_DIM).astype(F32),
                            bias, tq, lam_init)
        ymem = _mem_attention(proj3, mem_kv, min(512, s))

        rw = router_w[l].astype(F32)
        rwh = rw.astype(BF16)
        rwl = (rw - rwh.astype(F32)).astype(BF16)
        xn, tw, ti = _merge(
            xt, proj, ysb.reshape(t, BRANCH_WIDTH), ydf.reshape(t, BRANCH_WIDTH),
            ymem.reshape(t, BRANCH_WIDTH), b_gate[l].reshape(N_BRANCHES, D_MODEL).astype(F32),
            w_branch[l].astype(BF16), w_out[l].astype(BF16),
            ln1_g[l].reshape(1, d).astype(F32), ln1_b[l].reshape(1, d).astype(F32),
            rwh, rwl, router_b[l].reshape(1, N_EXPERTS).astype(F32), tok_tile)

        te, nv, tok, dst = _route(ti, t, MOE_TILE)
        yb = _moe(xn, te, nv, tok, dst, w_gate_up[l].astype(BF16),
                  b_gate_up[l].reshape(N_EXPERTS, 1, 2 * D_FF).astype(F32),
                  w_down[l].astype(BF16), b_down[l].reshape(N_EXPERTS, 1, D_MODEL).astype(F32),
                  MOE_TILE)
        xt = _combine(xn, tw, yb, ln2_g[l].reshape(1, d).astype(F32),
                      ln2_b[l].reshape(1, d).astype(F32), tok_tile)
    return xt.reshape(b, s, d)
```

```python
import functools
import math

import jax
import jax.numpy as jnp
from jax import lax
from jax.experimental import pallas as pl
from jax.experimental.pallas import tpu as pltpu

F32 = jnp.float32
BF16 = jnp.bfloat16

D_MODEL = 1024
DEPTH = 2
CHUNK = 64
BRANCH_WIDTH = D_MODEL // 2
SB_HEAD_DIM = 64
DIFF_HEAD_DIM = 64
DIFF_HEADS = BRANCH_WIDTH // (2 * DIFF_HEAD_DIM)
MEM_HEAD_DIM = 128
MEM_HEADS = BRANCH_WIDTH // MEM_HEAD_DIM
N_BRANCHES = 3
GATE_WIDTH = N_BRANCHES * D_MODEL
IN_WIDTH = 7 * BRANCH_WIDTH + GATE_WIDTH
NUM_BUCKETS = 32
MAX_DISTANCE = 128
N_EXPERTS = 32
TOP_K = 4
D_FF = D_MODEL
SWIGLU_LIMIT = 7.0
SWIGLU_ALPHA = 1.702
LN_EPS = 1e-5
RMS_EPS = 1e-5
DEEPNORM_ALPHA = (2 * DEPTH) ** 0.25
LOG2E = math.log2(math.e)

LANES = 128
SUBLANES = 8
NEG_BIG = -1e30

_GATE_BLK = 0
_SBQ_BLK = GATE_WIDTH // LANES
_SBK_BLK = _SBQ_BLK + 4
_SBV_BLK = _SBQ_BLK + 8
_DFQ_BLK = _SBQ_BLK + 12
_DFK_BLK = _SBQ_BLK + 16
_DFV_BLK = _SBQ_BLK + 20
_MEMQ_BLK = _SBQ_BLK + 24

VMEM_LIMIT = 56 * 1024 * 1024

ATT_TILE = 256
MOE_TILE = 256
TOK_TILE = 512


def _cparams(sem):
    return pltpu.CompilerParams(dimension_semantics=sem, vmem_limit_bytes=VMEM_LIMIT)


def _matmul_kernel(a_ref, w_ref, o_ref):
    a = a_ref[...].astype(BF16)
    o_ref[...] = jnp.dot(a, w_ref[...], preferred_element_type=F32).astype(o_ref.dtype)


def _matmul(a, w, tm, tn, name):
    m, k = a.shape
    n = w.shape[1]
    return pl.pallas_call(
        _matmul_kernel,
        out_shape=jax.ShapeDtypeStruct((m, n), BF16),
        grid=(m // tm, n // tn),
        in_specs=[pl.BlockSpec((tm, k), lambda i, j: (i, 0)),
                  pl.BlockSpec((k, tn), lambda i, j: (0, j))],
        out_specs=pl.BlockSpec((tm, tn), lambda i, j: (i, j)),
        compiler_params=_cparams(("parallel", "arbitrary")),
        name=name,
    )(a, w)


_NT = (((1,), (1,)), ((), ()))


def _sb_kernel(q_ref, k_ref, vt_ref, ut_ref, o_ref, c_ref, acc_ref, *, tq):
    i = pl.program_id(1)
    n_heads = 2 * (BRANCH_WIDTH // LANES)
    low = lax.broadcasted_iota(jnp.int32, (tq, LANES), 1) < SB_HEAD_DIM
    q_heads = []
    for hp in range(n_heads // 2):
        qs = q_ref[0, :, hp * LANES:(hp + 1) * LANES].astype(F32) * (-LOG2E * SB_HEAD_DIM ** -0.5)
        q_heads += [jnp.where(low, qs, 0.0).astype(BF16), jnp.where(low, 0.0, qs).astype(BF16)]
    ut = ut_ref[...]
    key = lax.broadcasted_iota(jnp.int32, (tq, tq), 0)
    qry = lax.broadcasted_iota(jnp.int32, (tq, tq), 1)
    earlier = key < qry

    c_ref[...] = jnp.zeros_like(c_ref)
    acc_ref[...] = jnp.zeros_like(acc_ref)

    def step(j, masked):
        start = pl.multiple_of(j * tq, tq)
        zns = []
        for h in range(n_heads):
            kb = k_ref[0, pl.ds(start, tq), (h // 2) * LANES:(h // 2 + 1) * LANES]
            zns.append(lax.dot_general(kb, q_heads[h], _NT, preferred_element_type=F32))
        l1ms = []
        for h in range(n_heads):
            zn = zns[h]
            neg_abs = lax.bitcast_convert_type(
                lax.bitcast_convert_type(zn, jnp.uint32) | jnp.uint32(0x80000000), F32)
            l1m = jnp.minimum(zn, 0.0) - jnp.log2(1.0 + jnp.exp2(neg_abs))
            if masked:
                l1m = jnp.where(earlier, l1m, 0.0)
            l1ms.append(l1m.astype(BF16))
        incls = [jnp.dot(ut, l1ms[h], preferred_element_type=F32) for h in range(n_heads)]
        a_s = []
        for h in range(n_heads):
            c = c_ref[h:h + 1, :]
            a = jnp.exp2(incls[h] + c - zns[h])
            if masked:
                a = jnp.where(earlier, a, 0.0)
            a_s.append(a.astype(BF16))
            c_ref[h:h + 1, :] = c + incls[h][0:1, :]
        for h in range(n_heads):
            vtb = vt_ref[0, j, (h // 2) * LANES:(h // 2 + 1) * LANES, :]
            acc_ref[h] += jnp.dot(vtb, a_s[h], preferred_element_type=F32)

    step(i, True)

    def body(n, carry):
        step(i - 1 - n, False)
        return carry

    lax.fori_loop(0, i, body, 0)
    sub = lax.broadcasted_iota(jnp.int32, (LANES, tq), 0)
    for hp in range(n_heads // 2):
        ot = jnp.where(sub < SB_HEAD_DIM, acc_ref[2 * hp], acc_ref[2 * hp + 1])
        o_ref[0, :, hp * LANES:(hp + 1) * LANES] = ot.T.astype(o_ref.dtype)


def _sb_attention(proj, vt, ut, tq):
    b, s, _ = proj.shape
    n_heads = 2 * (BRANCH_WIDTH // LANES)
    wblk = BRANCH_WIDTH // LANES
    kern = functools.partial(_sb_kernel, tq=tq)
    return pl.pallas_call(
        kern,
        out_shape=jax.ShapeDtypeStruct((b, s, BRANCH_WIDTH), BF16),
        grid=(b, s // tq),
        in_specs=[pl.BlockSpec((1, tq, BRANCH_WIDTH), lambda bi, i: (bi, i, _SBQ_BLK // wblk)),
                  pl.BlockSpec((1, s, BRANCH_WIDTH), lambda bi, i: (bi, 0, _SBK_BLK // wblk)),
                  pl.BlockSpec((1, s // tq, BRANCH_WIDTH, tq), lambda bi, i: (bi, 0, 0, 0)),
                  pl.BlockSpec((tq, tq), lambda bi, i: (0, 0))],
        out_specs=pl.BlockSpec((1, tq, BRANCH_WIDTH), lambda bi, i: (bi, i, 0)),
        scratch_shapes=[pltpu.VMEM((n_heads, tq), F32),
                        pltpu.VMEM((n_heads, LANES, tq), F32)],
        compiler_params=_cparams(("parallel", "arbitrary")),
        name="sb_attention",
    )(proj, proj, vt, ut)


def _df_kernel(lam_ref, g_ref, q_ref, k_ref, vt_ref, bias_ref, o_ref, mx_ref, den_ref, acc_ref,
               *, tq, lam_init):
    i = pl.program_id(1)
    n_maps = 2 * DIFF_HEADS
    low = lax.broadcasted_iota(jnp.int32, (tq, LANES), 1) < DIFF_HEAD_DIM
    q_maps = []
    for h in range(DIFF_HEADS):
        qs = q_ref[0, :, h * LANES:(h + 1) * LANES].astype(F32) * (LOG2E * DIFF_HEAD_DIM ** -0.5)
        q_maps += [jnp.where(low, qs, 0.0).astype(BF16), jnp.where(low, 0.0, qs).astype(BF16)]

    mx_ref[...] = jnp.full_like(mx_ref, NEG_BIG)
    den_ref[...] = jnp.zeros_like(den_ref)
    acc_ref[...] = jnp.zeros_like(acc_ref)

    def step(j, kind, off=None):
        start = pl.multiple_of(j * tq, tq)
        scs = []
        for m in range(n_maps):
            kb = k_ref[0, pl.ds(start, tq), (m // 2) * LANES:(m // 2 + 1) * LANES]
            scs.append(lax.dot_general(kb, q_maps[m], _NT, preferred_element_type=F32))
        ps, alphas = [], []
        for m in range(n_maps):
            sc = scs[m]
            if kind is not None:
                sc = sc + bias_ref[m, kind]
            if off is not None:
                sc = sc + off
            mx = mx_ref[m:m + 1, :]
            mx_new = jnp.maximum(mx, jnp.max(sc, axis=0, keepdims=True))
            alpha = jnp.exp2(mx - mx_new)
            p = jnp.exp2(sc - mx_new)
            den_ref[m:m + 1, :] = alpha * den_ref[m:m + 1, :] + jnp.sum(p, axis=0, keepdims=True)
            mx_ref[m:m + 1, :] = mx_new
            ps.append(p.astype(BF16))
            alphas.append(alpha)
        for m in range(n_maps):
            vtb = vt_ref[0, j, (m // 2) * LANES:(m // 2 + 1) * LANES, :]
            acc_ref[m] = alphas[m] * acc_ref[m] + jnp.dot(vtb, ps[m], preferred_element_type=F32)

    step(i, 0)
    step(jnp.maximum(i - 1, 0), 1, jnp.where(i == 0, NEG_BIG, 0.0).astype(F32))

    def body(n, carry):
        step(i - 2 - n, None)
        return carry

    lax.fori_loop(0, jnp.maximum(i - 1, 0), body, 0)

    lf = lam_ref[...]
    lam = (jnp.exp(jnp.sum(lf[0:1] * lf[1:2], axis=-1, keepdims=True))
           - jnp.exp(jnp.sum(lf[2:3] * lf[3:4], axis=-1, keepdims=True)) + lam_init)
    for h in range(DIFF_HEADS):
        m0, m1 = 2 * h, 2 * h + 1
        o = (acc_ref[m0] / den_ref[m0:m0 + 1, :]
             - lam * (acc_ref[m1] / den_ref[m1:m1 + 1, :]))
        o = o * lax.rsqrt(jnp.mean(o * o, axis=0, keepdims=True) + RMS_EPS)
        o_ref[0, :, h * LANES:(h + 1) * LANES] = (
            o.T * g_ref[...] * (1.0 - lam_init)).astype(o_ref.dtype)


def _df_attention(proj, vt, lam_rows, subln_g, bias, tq, lam_init):
    b, s, _ = proj.shape
    n_maps = 2 * DIFF_HEADS
    wblk = BRANCH_WIDTH // LANES
    kern = functools.partial(_df_kernel, tq=tq, lam_init=lam_init)
    return pl.pallas_call(
        kern,
        out_shape=jax.ShapeDtypeStruct((b, s, BRANCH_WIDTH), BF16),
        grid=(b, s // tq),
        in_specs=[pl.BlockSpec((4, DIFF_HEAD_DIM), lambda bi, i: (0, 0)),
                  pl.BlockSpec((1, LANES), lambda bi, i: (0, 0)),
                  pl.BlockSpec((1, tq, BRANCH_WIDTH), lambda bi, i: (bi, i, _DFQ_BLK // wblk)),
                  pl.BlockSpec((1, s, BRANCH_WIDTH), lambda bi, i: (bi, 0, _DFK_BLK // wblk)),
                  pl.BlockSpec((1, s // tq, BRANCH_WIDTH, tq), lambda bi, i: (bi, 0, 0, 0)),
                  pl.BlockSpec((n_maps, 2, tq, tq), lambda bi, i: (0, 0, 0, 0))],
        out_specs=pl.BlockSpec((1, tq, BRANCH_WIDTH), lambda bi, i: (bi, i, 0)),
        scratch_shapes=[pltpu.VMEM((n_maps, tq), F32),
                        pltpu.VMEM((n_maps, tq), F32),
                        pltpu.VMEM((n_maps, LANES, tq), F32)],
        compiler_params=_cparams(("parallel", "arbitrary")),
        name="diff_attention",
    )(lam_rows, subln_g, proj, proj, vt, bias)


def _key_major(v, tk):
    b, s, w = v.shape
    return v.reshape(b, s // tk, tk, w).swapaxes(2, 3)


def _t5_bucket(rel):
    half = NUM_BUCKETS // 2
    max_exact = half // 2
    n = jnp.abs(rel)
    nf = jnp.maximum(n, 1).astype(F32)
    large = max_exact + (jnp.log(nf / max_exact) / math.log(MAX_DISTANCE / max_exact)
                         * (half - max_exact)).astype(jnp.int32)
    large = jnp.minimum(large, half - 1)
    return jnp.where(rel > 0, half, 0) + jnp.where(n < max_exact, n, large)


def _bias_tiles(rel_bias, tq):
    table = rel_bias.astype(F32)
    r = jnp.arange(tq, dtype=jnp.int32)
    rel_diag = r[None, :] - r[:, None]
    far = table[_t5_bucket(jnp.full((), -(tq + 1), jnp.int32))]
    diag = (table[_t5_bucket(rel_diag)] - far) * LOG2E
    visible = (r[None, :] // CHUNK) <= (r[:, None] // CHUNK)
    diag = jnp.where(visible[..., None], diag, NEG_BIG)
    near = (table[_t5_bucket(rel_diag - tq)] - far) * LOG2E
    return jnp.stack([diag, near], axis=0).transpose(3, 0, 2, 1)


def _mem_kernel(q_ref, kv_ref, o_ref):
    for h in range(MEM_HEADS):
        lo = h * MEM_HEAD_DIM
        q = (q_ref[0, :, lo:lo + MEM_HEAD_DIM].astype(F32) * (MEM_HEAD_DIM ** -0.5)).astype(BF16)
        k = kv_ref[0, :, lo:lo + MEM_HEAD_DIM]
        v = kv_ref[0, :, BRANCH_WIDTH + lo:BRANCH_WIDTH + lo + MEM_HEAD_DIM]
        sc = lax.dot_general(q, k, (((1,), (1,)), ((), ())), preferred_element_type=F32)
        p = jnp.exp(sc - jnp.max(sc, axis=-1, keepdims=True))
        den = jnp.sum(p, axis=-1, keepdims=True)
        o = jnp.dot(p.astype(BF16), v, preferred_element_type=F32) / den
        o_ref[0, :, lo:lo + MEM_HEAD_DIM] = o.astype(o_ref.dtype)


def _mem_attention(proj, mem_kv, tq):
    b, s, _ = proj.shape
    m = mem_kv.shape[1]
    return pl.pallas_call(
        _mem_kernel,
        out_shape=jax.ShapeDtypeStruct((b, s, BRANCH_WIDTH), BF16),
        grid=(b, s // tq),
        in_specs=[pl.BlockSpec((1, tq, BRANCH_WIDTH),
                               lambda bi, i: (bi, i, _MEMQ_BLK * LANES // BRANCH_WIDTH)),
                  pl.BlockSpec((1, m, 2 * BRANCH_WIDTH), lambda bi, i: (bi, 0, 0))],
        out_specs=pl.BlockSpec((1, tq, BRANCH_WIDTH), lambda bi, i: (bi, i, 0)),
        compiler_params=_cparams(("parallel", "arbitrary")),
        name="mem_attention",
    )(proj, mem_kv)


def _layer_norm(r, g, b):
    mu = jnp.mean(r, axis=-1, keepdims=True)
    rc = r - mu
    var = jnp.mean(rc * rc, axis=-1, keepdims=True)
    return rc * lax.rsqrt(var + LN_EPS) * g + b


def _merge_kernel(x_ref, g0_ref, g1_ref, g2_ref, ysb_ref, ydf_ref, ymem_ref, bg_ref, wb_ref,
                  wo_ref, lng_ref, lnb_ref, rwh_ref, rwl_ref, rb_ref, xn_ref, tw_ref, ti_ref):
    merged = None
    for br, (g_ref, y_ref) in enumerate(((g0_ref, ysb_ref), (g1_ref, ydf_ref), (g2_ref, ymem_ref))):
        gate = jax.nn.sigmoid(g_ref[...].astype(F32) + bg_ref[br:br + 1, :])
        term = gate * jnp.dot(y_ref[...], wb_ref[br], preferred_element_type=F32)
        merged = term if merged is None else merged + term
    h = jnp.dot(merged.astype(BF16), wo_ref[...], preferred_element_type=F32)
    xn = _layer_norm(DEEPNORM_ALPHA * x_ref[...].astype(F32) + h, lng_ref[...], lnb_ref[...])
    xn_ref[...] = xn

    xh = xn.astype(BF16)
    xl = (xn - xh.astype(F32)).astype(BF16)
    logits = (jnp.dot(xh, rwh_ref[...], preferred_element_type=F32)
              + jnp.dot(xl, rwh_ref[...], preferred_element_type=F32)
              + jnp.dot(xh, rwl_ref[...], preferred_element_type=F32)) + rb_ref[...]
    tm = logits.shape[0]
    eid = lax.broadcasted_iota(jnp.int32, (tm, N_EXPERTS), 1).astype(F32)
    cur = logits
    vals, idxs = [], []
    for _ in range(TOP_K):
        best = jnp.max(cur, axis=-1, keepdims=True)
        idx = jnp.min(jnp.where(cur == best, eid, float(N_EXPERTS)), axis=-1, keepdims=True)
        vals.append(best)
        idxs.append(idx)
        cur = jnp.where(eid == idx, -jnp.inf, cur)
    exps = [jnp.exp(v - vals[0]) for v in vals]
    den = exps[0] + exps[1] + exps[2] + exps[3]
    lane = lax.broadcasted_iota(jnp.int32, (tm, LANES), 1)
    tw = jnp.zeros((tm, LANES), F32)
    ti = jnp.zeros((tm, LANES), F32)
    for k in range(TOP_K):
        tw = jnp.where(lane == k, exps[k] / den, tw)
        ti = jnp.where(lane == k, idxs[k], ti)
    tw_ref[...] = tw
    ti_ref[...] = ti.astype(jnp.int32)


def _merge(x, proj, ysb, ydf, ymem, b_gate, wb, wo, ln_g, ln_b, rwh, rwl, rb, tm):
    t = x.shape[0]
    tok = lambda blk: pl.BlockSpec((tm, blk[0]), lambda i, c=blk[1]: (i, c))
    full2 = lambda shape: pl.BlockSpec(shape, lambda i: (0, 0))
    return pl.pallas_call(
        _merge_kernel,
        out_shape=(jax.ShapeDtypeStruct((t, D_MODEL), F32),
                   jax.ShapeDtypeStruct((t, LANES), F32),
                   jax.ShapeDtypeStruct((t, LANES), jnp.int32)),
        grid=(t // tm,),
        in_specs=[tok((D_MODEL, 0)),
                  tok((D_MODEL, 0)), tok((D_MODEL, 1)), tok((D_MODEL, 2)),
                  tok((BRANCH_WIDTH, 0)), tok((BRANCH_WIDTH, 0)), tok((BRANCH_WIDTH, 0)),
                  full2((N_BRANCHES, D_MODEL)),
                  pl.BlockSpec((N_BRANCHES, BRANCH_WIDTH, D_MODEL), lambda i: (0, 0, 0)),
                  full2((D_MODEL, D_MODEL)),
                  full2((1, D_MODEL)), full2((1, D_MODEL)),
                  full2((D_MODEL, N_EXPERTS)), full2((D_MODEL, N_EXPERTS)),
                  full2((1, N_EXPERTS))],
        out_specs=(pl.BlockSpec((tm, D_MODEL), lambda i: (i, 0)),
                   pl.BlockSpec((tm, LANES), lambda i: (i, 0)),
                   pl.BlockSpec((tm, LANES), lambda i: (i, 0))),
        compiler_params=_cparams(("parallel",)),
        name="merge_ln_router",
    )(x, proj, proj, proj, ysb, ydf, ymem, b_gate, wb, wo, ln_g, ln_b, rwh, rwl, rb)


_MOE_LAG = 3


def _moe_kernel(plan_ref, tok_ref, dst_ref, x_hbm, wgu_ref, bgu_ref, wd_ref, bd_ref,
                y_hbm, xbuf0, xbuf1, ybuf0, ybuf1, sem_in, sem_out, *, tm, n_rows):
    s = pl.program_id(0)
    do_gather = plan_ref[1, s] > 0
    do_compute = plan_ref[2, s] > 0
    do_scatter = plan_ref[3, s] > 0
    do_drain = plan_ref[4, s] > 0
    steady = do_gather & do_compute & do_scatter & do_drain
    parity = s & 1

    @pl.when(s == 0)
    def _():
        ybuf1[...] = jnp.zeros((tm, D_MODEL), F32)
        init = pltpu.make_async_copy(ybuf1, y_hbm.at[pl.ds(n_rows, tm), :], sem_out.at[1])
        init.start()
        init.wait()

    def stages(par):
        x_new, y_old = (xbuf0, ybuf0) if par == 0 else (xbuf1, ybuf1)
        x_cur, y_cur = (xbuf1, ybuf1) if par == 0 else (xbuf0, ybuf0)
        new, cur = par, 1 - par

        def gather_row(r):
            t = tok_ref[0, 0, r]
            pltpu.make_async_copy(x_hbm.at[pl.ds(t, 1), :], x_new.at[pl.ds(r, 1), :],
                                  sem_in.at[new]).start()

        def scatter_row(r):
            d = dst_ref[0, 0, r]
            pltpu.make_async_copy(y_old.at[pl.ds(r, 1), :], y_hbm.at[pl.ds(d, 1), :],
                                  sem_out.at[new]).start()

        def wait_gather():
            pltpu.make_async_copy(x_hbm.at[pl.ds(0, tm), :], x_cur, sem_in.at[cur]).wait()

        def wait_scatter():
            pltpu.make_async_copy(y_cur, y_hbm.at[pl.ds(0, tm), :], sem_out.at[cur]).wait()

        def compute():
            xb = x_cur[...].astype(BF16)
            h = jnp.dot(xb, wgu_ref[0], preferred_element_type=F32) + bgu_ref[0]
            gate = jnp.minimum(h[:, :D_FF], SWIGLU_LIMIT)
            up = jnp.clip(h[:, D_FF:], -SWIGLU_LIMIT, SWIGLU_LIMIT)
            act = (up + 1.0) * (gate * jax.nn.sigmoid(SWIGLU_ALPHA * gate))
            y_cur[...] = jnp.dot(act.astype(BF16), wd_ref[0], preferred_element_type=F32) + bd_ref[0]

        return gather_row, scatter_row, wait_gather, wait_scatter, compute

    for par in range(2):
        gather_row, scatter_row, wait_gather, wait_scatter, compute = stages(par)

        @pl.when(steady & (parity == par))
        def _():
            wait_gather()
            wait_scatter()
            for r in range(tm):
                gather_row(r)
                scatter_row(r)
            compute()

        @pl.when(jnp.logical_not(steady) & (parity == par))
        def _():
            pl.when(do_compute)(wait_gather)
            pl.when(do_drain)(wait_scatter)

            @pl.when(do_gather)
            def _():
                lax.fori_loop(0, tm, lambda r, c: (gather_row(r), c)[1], 0)

            @pl.when(do_scatter)
            def _():
                lax.fori_loop(0, tm, lambda r, c: (scatter_row(r), c)[1], 0)

            pl.when(do_compute)(compute)


def _moe(xn, plan, tok, dst, wgu, bgu, wd, bd, tm):
    t = xn.shape[0]
    n_steps = tok.shape[0]
    n_rows = TOP_K * t
    kern = functools.partial(_moe_kernel, tm=tm, n_rows=n_rows)
    grid_spec = pltpu.PrefetchScalarGridSpec(
        num_scalar_prefetch=1,
        grid=(n_steps,),
        in_specs=[pl.BlockSpec((1, 1, tm), lambda s, plan: (s, 0, 0), memory_space=pltpu.SMEM),
                  pl.BlockSpec((1, 1, tm), lambda s, plan: (s, 0, 0), memory_space=pltpu.SMEM),
                  pl.BlockSpec(memory_space=pl.ANY),
                  pl.BlockSpec((1, D_MODEL, 2 * D_FF), lambda s, plan: (plan[0, s], 0, 0)),
                  pl.BlockSpec((1, 1, 2 * D_FF), lambda s, plan: (plan[0, s], 0, 0)),
                  pl.BlockSpec((1, D_FF, D_MODEL), lambda s, plan: (plan[0, s], 0, 0)),
                  pl.BlockSpec((1, 1, D_MODEL), lambda s, plan: (plan[0, s], 0, 0))],
        out_specs=pl.BlockSpec(memory_space=pl.ANY),
        scratch_shapes=[pltpu.VMEM((tm, D_MODEL), F32),
                        pltpu.VMEM((tm, D_MODEL), F32),
                        pltpu.VMEM((tm, D_MODEL), F32),
                        pltpu.VMEM((tm, D_MODEL), F32),
                        pltpu.SemaphoreType.DMA((2,)),
                        pltpu.SemaphoreType.DMA((2,))],
    )
    return pl.pallas_call(
        kern,
        out_shape=jax.ShapeDtypeStruct((n_rows + tm, D_MODEL), F32),
        grid_spec=grid_spec,
        compiler_params=_cparams(("arbitrary",)),
        name="moe_experts",
    )(plan, tok, dst, xn, wgu, bgu, wd, bd)


def _route(topi, t, tm):
    n_rows = TOP_K * t
    e_flat = topi[:, :TOP_K].T.reshape(-1)
    order = jnp.argsort(e_flat, stable=True).astype(jnp.int32)
    experts = jnp.arange(N_EXPERTS, dtype=jnp.int32)
    counts = jnp.sum((e_flat[:, None] == experts[None, :]).astype(jnp.int32), axis=0)
    off = jnp.cumsum(counts) - counts
    tiles = (counts + tm - 1) // tm
    tile_end = jnp.cumsum(tiles)
    tile_off = tile_end - tiles
    total = tile_end[-1]
    n_tiles = n_rows // tm + N_EXPERTS
    g = jnp.arange(n_tiles, dtype=jnp.int32)
    te = jnp.minimum(jnp.searchsorted(tile_end, g, side="right"), N_EXPERTS - 1).astype(jnp.int32)
    r = jnp.arange(tm, dtype=jnp.int32)
    start = (g - tile_off[te]) * tm
    nv = jnp.where(g < total, jnp.clip(counts[te] - start, 0, tm), 0).astype(jnp.int32)
    local = start[:, None] + r[None, :]
    valid = r[None, :] < nv[:, None]
    src = jnp.clip(off[te][:, None] + local, 0, n_rows - 1)
    f = order[src]
    tok = jnp.where(valid, f % t, 0).astype(jnp.int32)
    dst = jnp.where(valid, f, n_rows + r[None, :]).astype(jnp.int32)

    def lagged(a, lag):
        return jnp.pad(a, ((lag, _MOE_LAG - lag),) + ((0, 0),) * (a.ndim - 1))

    plan = jnp.stack([jnp.pad(te, (1, _MOE_LAG - 1), mode="edge"),
                      lagged(nv, 0), lagged(nv, 1), lagged(nv, 2), lagged(nv, 3)])
    n_steps = n_tiles + _MOE_LAG
    return (plan, lagged(tok, 0).reshape(n_steps, 1, tm), lagged(dst, 2).reshape(n_steps, 1, tm))


def _combine_kernel(xn_ref, tw_ref, y0_ref, y1_ref, y2_ref, y3_ref, lng_ref, lnb_ref, o_ref):
    tw = tw_ref[...]
    f = None
    for k, y_ref in enumerate((y0_ref, y1_ref, y2_ref, y3_ref)):
        term = tw[:, k:k + 1] * y_ref[...]
        f = term if f is None else f + term
    o_ref[...] = _layer_norm(DEEPNORM_ALPHA * xn_ref[...] + f, lng_ref[...], lnb_ref[...])


def _combine(xn, tw, yb, ln_g, ln_b, tm):
    t = xn.shape[0]
    nblk = t // tm
    ysp = lambda k: pl.BlockSpec((tm, D_MODEL), lambda i, k=k: (k * nblk + i, 0))
    return pl.pallas_call(
        _combine_kernel,
        out_shape=jax.ShapeDtypeStruct((t, D_MODEL), F32),
        grid=(nblk,),
        in_specs=[pl.BlockSpec((tm, D_MODEL), lambda i: (i, 0)),
                  pl.BlockSpec((tm, LANES), lambda i: (i, 0)),
                  ysp(0), ysp(1), ysp(2), ysp(3),
                  pl.BlockSpec((1, D_MODEL), lambda i: (0, 0)),
                  pl.BlockSpec((1, D_MODEL), lambda i: (0, 0))],
        out_specs=pl.BlockSpec((tm, D_MODEL), lambda i: (i, 0)),
        compiler_params=_cparams(("parallel",)),
        name="combine_ln",
    )(xn, tw, yb, yb, yb, yb, ln_g, ln_b)


def kernel(x, mem, w_in, b_gate, diff_lambda, diff_subln_g, rel_bias, w_mem_kv, w_branch, w_out,
           ln1_g, ln1_b, router_w, router_b, w_gate_up, b_gate_up, w_down, b_down, ln2_g, ln2_b):
    b, s, d = x.shape
    t = b * s
    m = mem.shape[1]
    tq = min(ATT_TILE, s)
    tok_tile = min(TOK_TILE, t)

    idx = jnp.arange(tq, dtype=jnp.int32)
    ut = (idx[None, :] >= idx[:, None]).astype(BF16)
    bias = _bias_tiles(rel_bias, tq)
    mem2 = mem.reshape(b * m, d)
    xt = x.reshape(t, d)

    for l in range(DEPTH):
        w_in_l = jnp.concatenate([w_in[l][:, 7 * BRANCH_WIDTH:], w_in[l][:, :7 * BRANCH_WIDTH]],
                                 axis=1).astype(BF16)
        proj = _matmul(xt, w_in_l, min(1024, t), 512, "in_proj")
        proj3 = proj.reshape(b, s, IN_WIDTH)
        mem_kv = _matmul(mem2, w_mem_kv[l].astype(BF16), min(1024, b * m), 512, "mem_kv")
        mem_kv = mem_kv.reshape(b, m, 2 * BRANCH_WIDTH)

        lam_init = 0.8 - 0.6 * math.exp(-0.3 * l)
        sbv_t = _key_major(proj3[:, :, _SBV_BLK * LANES:_SBV_BLK * LANES + BRANCH_WIDTH], tq)
        dfv_t = _key_major(proj3[:, :, _DFV_BLK * LANES:_DFV_BLK * LANES + BRANCH_WIDTH], tq)
        ysb = _sb_attention(proj3, sbv_t, ut, tq)
        ydf = _df_attention(proj3, dfv_t, diff_lambda[l].astype(F32),
                            diff_subln_g[l].reshape(1, 2 * DIFF_HEAD_DIM).astype(F32),
                            bias, tq, lam_init)
        ymem = _mem_attention(proj3, mem_kv, min(512, s))

        rw = router_w[l].astype(F32)
        rwh = rw.astype(BF16)
        rwl = (rw - rwh.astype(F32)).astype(BF16)
        xn, tw, ti = _merge(
            xt, proj, ysb.reshape(t, BRANCH_WIDTH), ydf.reshape(t, BRANCH_WIDTH),
            ymem.reshape(t, BRANCH_WIDTH), b_gate[l].reshape(N_BRANCHES, D_MODEL).astype(F32),
            w_branch[l].astype(BF16), w_out[l].astype(BF16),
            ln1_g[l].reshape(1, d).astype(F32), ln1_b[l].reshape(1, d).astype(F32),
            rwh, rwl, router_b[l].reshape(1, N_EXPERTS).astype(F32), tok_tile)

        plan, tok, dst = _route(ti, t, MOE_TILE)
        yb = _moe(xn, plan, tok, dst, w_gate_up[l].astype(BF16),
                  b_gate_up[l].reshape(N_EXPERTS, 1, 2 * D_FF).astype(F32),
                  w_down[l].astype(BF16), b_down[l].reshape(N_EXPERTS, 1, D_MODEL).astype(F32),
                  MOE_TILE)
        xt = _combine(xn, tw, yb, ln2_g[l].reshape(1, d).astype(F32),
                      ln2_b[l].reshape(1, d).astype(F32), tok_tile)
    return xt.reshape(b, s, d)
```

```python
import functools
import math

import jax
import jax.numpy as jnp
from jax import lax
from jax.experimental import pallas as pl
from jax.experimental.pallas import tpu as pltpu

F32 = jnp.float32
BF16 = jnp.bfloat16

D_MODEL = 1024
DEPTH = 2
CHUNK = 64
BRANCH_WIDTH = D_MODEL // 2
SB_HEAD_DIM = 64
DIFF_HEAD_DIM = 64
DIFF_HEADS = BRANCH_WIDTH // (2 * DIFF_HEAD_DIM)
MEM_HEAD_DIM = 128
MEM_HEADS = BRANCH_WIDTH // MEM_HEAD_DIM
N_BRANCHES = 3
GATE_WIDTH = N_BRANCHES * D_MODEL
IN_WIDTH = 7 * BRANCH_WIDTH + GATE_WIDTH
NUM_BUCKETS = 32
MAX_DISTANCE = 128
N_EXPERTS = 32
TOP_K = 4
D_FF = D_MODEL
SWIGLU_LIMIT = 7.0
SWIGLU_ALPHA = 1.702
LN_EPS = 1e-5
RMS_EPS = 1e-5
DEEPNORM_ALPHA = (2 * DEPTH) ** 0.25
LOG2E = math.log2(math.e)

LANES = 128
SUBLANES = 8
ROW_CHUNKS = D_MODEL // LANES
assert ROW_CHUNKS == SUBLANES
NEG_BIG = -1e30

_GATE_BLK = 0
_SBQ_BLK = GATE_WIDTH // LANES
_SBK_BLK = _SBQ_BLK + 4
_SBV_BLK = _SBQ_BLK + 8
_DFQ_BLK = _SBQ_BLK + 12
_DFK_BLK = _SBQ_BLK + 16
_DFV_BLK = _SBQ_BLK + 20
_MEMQ_BLK = _SBQ_BLK + 24

VMEM_LIMIT = 56 * 1024 * 1024

ATT_TILE = 256
MOE_TILE = 256
TOK_TILE = 512


def _cparams(sem):
    return pltpu.CompilerParams(dimension_semantics=sem, vmem_limit_bytes=VMEM_LIMIT)


def _load_row_tiles(ref, n):
    return jnp.concatenate([ref[pl.ds(c, n, stride=ROW_CHUNKS), :] for c in range(ROW_CHUNKS)],
                           axis=1)


def _store_row_tiles(ref, value):
    n = value.shape[0]
    for c in range(ROW_CHUNKS):
        ref[pl.ds(c, n, stride=ROW_CHUNKS), :] = value[:, c * LANES:(c + 1) * LANES]


def _matmul_kernel(a_ref, w_ref, o_ref):
    a = a_ref[...].astype(BF16)
    o_ref[...] = jnp.dot(a, w_ref[...], preferred_element_type=F32).astype(o_ref.dtype)


def _matmul(a, w, tm, tn, name):
    m, k = a.shape
    n = w.shape[1]
    return pl.pallas_call(
        _matmul_kernel,
        out_shape=jax.ShapeDtypeStruct((m, n), BF16),
        grid=(m // tm, n // tn),
        in_specs=[pl.BlockSpec((tm, k), lambda i, j: (i, 0)),
                  pl.BlockSpec((k, tn), lambda i, j: (0, j))],
        out_specs=pl.BlockSpec((tm, tn), lambda i, j: (i, j)),
        compiler_params=_cparams(("parallel", "arbitrary")),
        name=name,
    )(a, w)


_NT = (((1,), (1,)), ((), ()))


def _sb_kernel(q_ref, k_ref, vt_ref, ut_ref, o_ref, c_ref, acc_ref, *, tq):
    i = pl.program_id(1)
    n_heads = 2 * (BRANCH_WIDTH // LANES)
    low = lax.broadcasted_iota(jnp.int32, (tq, LANES), 1) < SB_HEAD_DIM
    q_heads = []
    for hp in range(n_heads // 2):
        qs = q_ref[0, :, hp * LANES:(hp + 1) * LANES].astype(F32) * (-LOG2E * SB_HEAD_DIM ** -0.5)
        q_heads += [jnp.where(low, qs, 0.0).astype(BF16), jnp.where(low, 0.0, qs).astype(BF16)]
    ut = ut_ref[...]
    key = lax.broadcasted_iota(jnp.int32, (tq, tq), 0)
    qry = lax.broadcasted_iota(jnp.int32, (tq, tq), 1)
    earlier = key < qry

    c_ref[...] = jnp.zeros_like(c_ref)
    acc_ref[...] = jnp.zeros_like(acc_ref)

    def step(j, masked):
        start = pl.multiple_of(j * tq, tq)
        zns = []
        for h in range(n_heads):
            kb = k_ref[0, pl.ds(start, tq), (h // 2) * LANES:(h // 2 + 1) * LANES]
            zns.append(lax.dot_general(kb, q_heads[h], _NT, preferred_element_type=F32))
        l1ms = []
        for h in range(n_heads):
            zn = zns[h]
            neg_abs = lax.bitcast_convert_type(
                lax.bitcast_convert_type(zn, jnp.uint32) | jnp.uint32(0x80000000), F32)
            l1m = jnp.minimum(zn, 0.0) - jnp.log2(1.0 + jnp.exp2(neg_abs))
            if masked:
                l1m = jnp.where(earlier, l1m, 0.0)
            l1ms.append(l1m.astype(BF16))
        incls = [jnp.dot(ut, l1ms[h], preferred_element_type=F32) for h in range(n_heads)]
        a_s = []
        for h in range(n_heads):
            c = c_ref[h:h + 1, :]
            a = jnp.exp2(incls[h] + c - zns[h])
            if masked:
                a = jnp.where(earlier, a, 0.0)
            a_s.append(a.astype(BF16))
            c_ref[h:h + 1, :] = c + incls[h][0:1, :]
        for h in range(n_heads):
            vtb = vt_ref[0, j, (h // 2) * LANES:(h // 2 + 1) * LANES, :]
            acc_ref[h] += jnp.dot(vtb, a_s[h], preferred_element_type=F32)

    step(i, True)

    def body(n, carry):
        step(i - 1 - n, False)
        return carry

    lax.fori_loop(0, i, body, 0)
    sub = lax.broadcasted_iota(jnp.int32, (LANES, tq), 0)
    for hp in range(n_heads // 2):
        ot = jnp.where(sub < SB_HEAD_DIM, acc_ref[2 * hp], acc_ref[2 * hp + 1])
        o_ref[0, :, hp * LANES:(hp + 1) * LANES] = ot.T.astype(o_ref.dtype)


def _sb_attention(proj, vt, ut, tq):
    b, s, _ = proj.shape
    n_heads = 2 * (BRANCH_WIDTH // LANES)
    wblk = BRANCH_WIDTH // LANES
    kern = functools.partial(_sb_kernel, tq=tq)
    return pl.pallas_call(
        kern,
        out_shape=jax.ShapeDtypeStruct((b, s, BRANCH_WIDTH), BF16),
        grid=(b, s // tq),
        in_specs=[pl.BlockSpec((1, tq, BRANCH_WIDTH), lambda bi, i: (bi, i, _SBQ_BLK // wblk)),
                  pl.BlockSpec((1, s, BRANCH_WIDTH), lambda bi, i: (bi, 0, _SBK_BLK // wblk)),
                  pl.BlockSpec((1, s // tq, BRANCH_WIDTH, tq), lambda bi, i: (bi, 0, 0, 0)),
                  pl.BlockSpec((tq, tq), lambda bi, i: (0, 0))],
        out_specs=pl.BlockSpec((1, tq, BRANCH_WIDTH), lambda bi, i: (bi, i, 0)),
        scratch_shapes=[pltpu.VMEM((n_heads, tq), F32),
                        pltpu.VMEM((n_heads, LANES, tq), F32)],
        compiler_params=_cparams(("parallel", "arbitrary")),
        name="sb_attention",
    )(proj, proj, vt, ut)


def _df_kernel(lam_ref, g_ref, q_ref, k_ref, vt_ref, bias_ref, o_ref, mx_ref, den_ref, acc_ref,
               *, tq, lam_init):
    i = pl.program_id(1)
    n_maps = 2 * DIFF_HEADS
    low = lax.broadcasted_iota(jnp.int32, (tq, LANES), 1) < DIFF_HEAD_DIM
    q_maps = []
    for h in range(DIFF_HEADS):
        qs = q_ref[0, :, h * LANES:(h + 1) * LANES].astype(F32) * (LOG2E * DIFF_HEAD_DIM ** -0.5)
        q_maps += [jnp.where(low, qs, 0.0).astype(BF16), jnp.where(low, 0.0, qs).astype(BF16)]

    mx_ref[...] = jnp.full_like(mx_ref, NEG_BIG)
    den_ref[...] = jnp.zeros_like(den_ref)
    acc_ref[...] = jnp.zeros_like(acc_ref)

    def step(j, kind, off=None):
        start = pl.multiple_of(j * tq, tq)
        scs = []
        for m in range(n_maps):
            kb = k_ref[0, pl.ds(start, tq), (m // 2) * LANES:(m // 2 + 1) * LANES]
            scs.append(lax.dot_general(kb, q_maps[m], _NT, preferred_element_type=F32))
        ps, alphas = [], []
        for m in range(n_maps):
            sc = scs[m]
            if kind is not None:
                sc = sc + bias_ref[m, kind]
            if off is not None:
                sc = sc + off
            mx = mx_ref[m:m + 1, :]
            mx_new = jnp.maximum(mx, jnp.max(sc, axis=0, keepdims=True))
            alpha = jnp.exp2(mx - mx_new)
            p = jnp.exp2(sc - mx_new)
            den_ref[m:m + 1, :] = alpha * den_ref[m:m + 1, :] + jnp.sum(p, axis=0, keepdims=True)
            mx_ref[m:m + 1, :] = mx_new
            ps.append(p.astype(BF16))
            alphas.append(alpha)
        for m in range(n_maps):
            vtb = vt_ref[0, j, (m // 2) * LANES:(m // 2 + 1) * LANES, :]
            acc_ref[m] = alphas[m] * acc_ref[m] + jnp.dot(vtb, ps[m], preferred_element_type=F32)

    step(i, 0)
    step(jnp.maximum(i - 1, 0), 1, jnp.where(i == 0, NEG_BIG, 0.0).astype(F32))

    def body(n, carry):
        step(i - 2 - n, None)
        return carry

    lax.fori_loop(0, jnp.maximum(i - 1, 0), body, 0)

    lf = lam_ref[...]
    lam = (jnp.exp(jnp.sum(lf[0:1] * lf[1:2], axis=-1, keepdims=True))
           - jnp.exp(jnp.sum(lf[2:3] * lf[3:4], axis=-1, keepdims=True)) + lam_init)
    for h in range(DIFF_HEADS):
        m0, m1 = 2 * h, 2 * h + 1
        o = (acc_ref[m0] / den_ref[m0:m0 + 1, :]
             - lam * (acc_ref[m1] / den_ref[m1:m1 + 1, :]))
        o = o * lax.rsqrt(jnp.mean(o * o, axis=0, keepdims=True) + RMS_EPS)
        o_ref[0, :, h * LANES:(h + 1) * LANES] = (
            o.T * g_ref[...] * (1.0 - lam_init)).astype(o_ref.dtype)


def _df_attention(proj, vt, lam_rows, subln_g, bias, tq, lam_init):
    b, s, _ = proj.shape
    n_maps = 2 * DIFF_HEADS
    wblk = BRANCH_WIDTH // LANES
    kern = functools.partial(_df_kernel, tq=tq, lam_init=lam_init)
    return pl.pallas_call(
        kern,
        out_shape=jax.ShapeDtypeStruct((b, s, BRANCH_WIDTH), BF16),
        grid=(b, s // tq),
        in_specs=[pl.BlockSpec((4, DIFF_HEAD_DIM), lambda bi, i: (0, 0)),
                  pl.BlockSpec((1, LANES), lambda bi, i: (0, 0)),
                  pl.BlockSpec((1, tq, BRANCH_WIDTH), lambda bi, i: (bi, i, _DFQ_BLK // wblk)),
                  pl.BlockSpec((1, s, BRANCH_WIDTH), lambda bi, i: (bi, 0, _DFK_BLK // wblk)),
                  pl.BlockSpec((1, s // tq, BRANCH_WIDTH, tq), lambda bi, i: (bi, 0, 0, 0)),
                  pl.BlockSpec((n_maps, 2, tq, tq), lambda bi, i: (0, 0, 0, 0))],
        out_specs=pl.BlockSpec((1, tq, BRANCH_WIDTH), lambda bi, i: (bi, i, 0)),
        scratch_shapes=[pltpu.VMEM((n_maps, tq), F32),
                        pltpu.VMEM((n_maps, tq), F32),
                        pltpu.VMEM((n_maps, LANES, tq), F32)],
        compiler_params=_cparams(("parallel", "arbitrary")),
        name="diff_attention",
    )(lam_rows, subln_g, proj, proj, vt, bias)


def _key_major(v, tk):
    b, s, w = v.shape
    return v.reshape(b, s // tk, tk, w).swapaxes(2, 3)


def _t5_bucket(rel):
    half = NUM_BUCKETS // 2
    max_exact = half // 2
    n = jnp.abs(rel)
    nf = jnp.maximum(n, 1).astype(F32)
    large = max_exact + (jnp.log(nf / max_exact) / math.log(MAX_DISTANCE / max_exact)
                         * (half - max_exact)).astype(jnp.int32)
    large = jnp.minimum(large, half - 1)
    return jnp.where(rel > 0, half, 0) + jnp.where(n < max_exact, n, large)


def _bias_tiles(rel_bias, tq):
    table = rel_bias.astype(F32)
    r = jnp.arange(tq, dtype=jnp.int32)
    rel_diag = r[None, :] - r[:, None]
    far = table[_t5_bucket(jnp.full((), -(tq + 1), jnp.int32))]
    diag = (table[_t5_bucket(rel_diag)] - far) * LOG2E
    visible = (r[None, :] // CHUNK) <= (r[:, None] // CHUNK)
    diag = jnp.where(visible[..., None], diag, NEG_BIG)
    near = (table[_t5_bucket(rel_diag - tq)] - far) * LOG2E
    return jnp.stack([diag, near], axis=0).transpose(3, 0, 2, 1)


def _mem_kernel(q_ref, kv_ref, o_ref):
    for h in range(MEM_HEADS):
        lo = h * MEM_HEAD_DIM
        q = (q_ref[0, :, lo:lo + MEM_HEAD_DIM].astype(F32) * (MEM_HEAD_DIM ** -0.5)).astype(BF16)
        k = kv_ref[0, :, lo:lo + MEM_HEAD_DIM]
        v = kv_ref[0, :, BRANCH_WIDTH + lo:BRANCH_WIDTH + lo + MEM_HEAD_DIM]
        sc = lax.dot_general(q, k, (((1,), (1,)), ((), ())), preferred_element_type=F32)
        p = jnp.exp(sc - jnp.max(sc, axis=-1, keepdims=True))
        den = jnp.sum(p, axis=-1, keepdims=True)
        o = jnp.dot(p.astype(BF16), v, preferred_element_type=F32) / den
        o_ref[0, :, lo:lo + MEM_HEAD_DIM] = o.astype(o_ref.dtype)


def _mem_attention(proj, mem_kv, tq):
    b, s, _ = proj.shape
    m = mem_kv.shape[1]
    return pl.pallas_call(
        _mem_kernel,
        out_shape=jax.ShapeDtypeStruct((b, s, BRANCH_WIDTH), BF16),
        grid=(b, s // tq),
        in_specs=[pl.BlockSpec((1, tq, BRANCH_WIDTH),
                               lambda bi, i: (bi, i, _MEMQ_BLK * LANES // BRANCH_WIDTH)),
                  pl.BlockSpec((1, m, 2 * BRANCH_WIDTH), lambda bi, i: (bi, 0, 0))],
        out_specs=pl.BlockSpec((1, tq, BRANCH_WIDTH), lambda bi, i: (bi, i, 0)),
        compiler_params=_cparams(("parallel", "arbitrary")),
        name="mem_attention",
    )(proj, mem_kv)


def _layer_norm(r, g, b):
    mu = jnp.mean(r, axis=-1, keepdims=True)
    rc = r - mu
    var = jnp.mean(rc * rc, axis=-1, keepdims=True)
    return rc * lax.rsqrt(var + LN_EPS) * g + b


def _merge_kernel(x_ref, g0_ref, g1_ref, g2_ref, ysb_ref, ydf_ref, ymem_ref, bg_ref, wb_ref,
                  wo_ref, lng_ref, lnb_ref, rwh_ref, rwl_ref, rb_ref, xn_ref, xg_ref, tw_ref, ti_ref):
    merged = None
    for br, (g_ref, y_ref) in enumerate(((g0_ref, ysb_ref), (g1_ref, ydf_ref), (g2_ref, ymem_ref))):
        gate = jax.nn.sigmoid(g_ref[...].astype(F32) + bg_ref[br:br + 1, :])
        term = gate * jnp.dot(y_ref[...], wb_ref[br], preferred_element_type=F32)
        merged = term if merged is None else merged + term
    h = jnp.dot(merged.astype(BF16), wo_ref[...], preferred_element_type=F32)
    xn = _layer_norm(DEEPNORM_ALPHA * x_ref[...].astype(F32) + h, lng_ref[...], lnb_ref[...])
    xn_ref[...] = xn
    _store_row_tiles(xg_ref, xn)

    xh = xn.astype(BF16)
    xl = (xn - xh.astype(F32)).astype(BF16)
    logits = (jnp.dot(xh, rwh_ref[...], preferred_element_type=F32)
              + jnp.dot(xl, rwh_ref[...], preferred_element_type=F32)
              + jnp.dot(xh, rwl_ref[...], preferred_element_type=F32)) + rb_ref[...]
    tm = logits.shape[0]
    eid = lax.broadcasted_iota(jnp.int32, (tm, N_EXPERTS), 1).astype(F32)
    cur = logits
    vals, idxs = [], []
    for _ in range(TOP_K):
        best = jnp.max(cur, axis=-1, keepdims=True)
        idx = jnp.min(jnp.where(cur == best, eid, float(N_EXPERTS)), axis=-1, keepdims=True)
        vals.append(best)
        idxs.append(idx)
        cur = jnp.where(eid == idx, -jnp.inf, cur)
    exps = [jnp.exp(v - vals[0]) for v in vals]
    den = exps[0] + exps[1] + exps[2] + exps[3]
    lane = lax.broadcasted_iota(jnp.int32, (tm, LANES), 1)
    tw = jnp.zeros((tm, LANES), F32)
    ti = jnp.zeros((tm, LANES), F32)
    for k in range(TOP_K):
        tw = jnp.where(lane == k, exps[k] / den, tw)
        ti = jnp.where(lane == k, idxs[k], ti)
    tw_ref[...] = tw
    ti_ref[...] = ti.astype(jnp.int32)


def _merge(x, proj, ysb, ydf, ymem, b_gate, wb, wo, ln_g, ln_b, rwh, rwl, rb, tm):
    t = x.shape[0]
    tok = lambda blk: pl.BlockSpec((tm, blk[0]), lambda i, c=blk[1]: (i, c))
    full2 = lambda shape: pl.BlockSpec(shape, lambda i: (0, 0))
    return pl.pallas_call(
        _merge_kernel,
        out_shape=(jax.ShapeDtypeStruct((t, D_MODEL), F32),
                   jax.ShapeDtypeStruct((t * ROW_CHUNKS, LANES), F32),
                   jax.ShapeDtypeStruct((t, LANES), F32),
                   jax.ShapeDtypeStruct((t, LANES), jnp.int32)),
        grid=(t // tm,),
        in_specs=[tok((D_MODEL, 0)),
                  tok((D_MODEL, 0)), tok((D_MODEL, 1)), tok((D_MODEL, 2)),
                  tok((BRANCH_WIDTH, 0)), tok((BRANCH_WIDTH, 0)), tok((BRANCH_WIDTH, 0)),
                  full2((N_BRANCHES, D_MODEL)),
                  pl.BlockSpec((N_BRANCHES, BRANCH_WIDTH, D_MODEL), lambda i: (0, 0, 0)),
                  full2((D_MODEL, D_MODEL)),
                  full2((1, D_MODEL)), full2((1, D_MODEL)),
                  full2((D_MODEL, N_EXPERTS)), full2((D_MODEL, N_EXPERTS)),
                  full2((1, N_EXPERTS))],
        out_specs=(pl.BlockSpec((tm, D_MODEL), lambda i: (i, 0)),
                   pl.BlockSpec((tm * ROW_CHUNKS, LANES), lambda i: (i, 0)),
                   pl.BlockSpec((tm, LANES), lambda i: (i, 0)),
                   pl.BlockSpec((tm, LANES), lambda i: (i, 0))),
        compiler_params=_cparams(("parallel",)),
        name="merge_ln_router",
    )(x, proj, proj, proj, ysb, ydf, ymem, b_gate, wb, wo, ln_g, ln_b, rwh, rwl, rb)


_MOE_LAG = 3


def _moe_kernel(plan_ref, tok_ref, dst_ref, x_hbm, wgu_ref, bgu_ref, wd_ref, bd_ref,
                y_hbm, xbuf0, xbuf1, ybuf0, ybuf1, sem_in, sem_out, *, tm, n_rows):
    s = pl.program_id(0)
    do_gather = plan_ref[1, s] > 0
    do_compute = plan_ref[2, s] > 0
    do_scatter = plan_ref[3, s] > 0
    do_drain = plan_ref[4, s] > 0
    steady = do_gather & do_compute & do_scatter & do_drain
    parity = s & 1

    @pl.when(s == 0)
    def _():
        ybuf1[...] = jnp.zeros_like(ybuf1)
        init = pltpu.make_async_copy(ybuf1, y_hbm.at[pl.ds(n_rows * ROW_CHUNKS, tm * ROW_CHUNKS), :],
                                     sem_out.at[1])
        init.start()
        init.wait()

    def stages(par):
        x_new, y_old = (xbuf0, ybuf0) if par == 0 else (xbuf1, ybuf1)
        x_cur, y_cur = (xbuf1, ybuf1) if par == 0 else (xbuf0, ybuf0)
        new, cur = par, 1 - par

        def gather_row(r):
            t = pl.multiple_of(tok_ref[0, 0, r], ROW_CHUNKS)
            pltpu.make_async_copy(x_hbm.at[pl.ds(t, ROW_CHUNKS), :],
                                  x_new.at[pl.ds(r * ROW_CHUNKS, ROW_CHUNKS), :],
                                  sem_in.at[new]).start()

        def scatter_row(r):
            d = pl.multiple_of(dst_ref[0, 0, r], ROW_CHUNKS)
            pltpu.make_async_copy(y_old.at[pl.ds(r * ROW_CHUNKS, ROW_CHUNKS), :],
                                  y_hbm.at[pl.ds(d, ROW_CHUNKS), :], sem_out.at[new]).start()

        def wait_gather():
            pltpu.make_async_copy(x_hbm.at[pl.ds(0, tm * ROW_CHUNKS), :], x_cur, sem_in.at[cur]).wait()

        def wait_scatter():
            pltpu.make_async_copy(y_cur, y_hbm.at[pl.ds(0, tm * ROW_CHUNKS), :], sem_out.at[cur]).wait()

        def compute():
            xb = _load_row_tiles(x_cur, tm).astype(BF16)
            h = jnp.dot(xb, wgu_ref[0], preferred_element_type=F32) + bgu_ref[0]
            gate = jnp.minimum(h[:, :D_FF], SWIGLU_LIMIT)
            up = jnp.clip(h[:, D_FF:], -SWIGLU_LIMIT, SWIGLU_LIMIT)
            act = (up + 1.0) * (gate * jax.nn.sigmoid(SWIGLU_ALPHA * gate))
            _store_row_tiles(
                y_cur, jnp.dot(act.astype(BF16), wd_ref[0], preferred_element_type=F32) + bd_ref[0])

        return gather_row, scatter_row, wait_gather, wait_scatter, compute

    for par in range(2):
        gather_row, scatter_row, wait_gather, wait_scatter, compute = stages(par)

        @pl.when(steady & (parity == par))
        def _():
            wait_gather()
            wait_scatter()
            for r in range(tm):
                gather_row(r)
                scatter_row(r)
            compute()

        @pl.when(jnp.logical_not(steady) & (parity == par))
        def _():
            pl.when(do_compute)(wait_gather)
            pl.when(do_drain)(wait_scatter)

            @pl.when(do_gather)
            def _():
                lax.fori_loop(0, tm, lambda r, c: (gather_row(r), c)[1], 0)

            @pl.when(do_scatter)
            def _():
                lax.fori_loop(0, tm, lambda r, c: (scatter_row(r), c)[1], 0)

            pl.when(do_compute)(compute)


def _moe(xg, plan, tok, dst, wgu, bgu, wd, bd, tm):
    t = xg.shape[0] // ROW_CHUNKS
    n_steps = tok.shape[0]
    n_rows = TOP_K * t
    kern = functools.partial(_moe_kernel, tm=tm, n_rows=n_rows)
    grid_spec = pltpu.PrefetchScalarGridSpec(
        num_scalar_prefetch=1,
        grid=(n_steps,),
        in_specs=[pl.BlockSpec((1, 1, tm), lambda s, plan: (s, 0, 0), memory_space=pltpu.SMEM),
                  pl.BlockSpec((1, 1, tm), lambda s, plan: (s, 0, 0), memory_space=pltpu.SMEM),
                  pl.BlockSpec(memory_space=pl.ANY),
                  pl.BlockSpec((1, D_MODEL, 2 * D_FF), lambda s, plan: (plan[0, s], 0, 0)),
                  pl.BlockSpec((1, 1, 2 * D_FF), lambda s, plan: (plan[0, s], 0, 0)),
                  pl.BlockSpec((1, D_FF, D_MODEL), lambda s, plan: (plan[0, s], 0, 0)),
                  pl.BlockSpec((1, 1, D_MODEL), lambda s, plan: (plan[0, s], 0, 0))],
        out_specs=pl.BlockSpec(memory_space=pl.ANY),
        scratch_shapes=[pltpu.VMEM((tm * ROW_CHUNKS, LANES), F32),
                        pltpu.VMEM((tm * ROW_CHUNKS, LANES), F32),
                        pltpu.VMEM((tm * ROW_CHUNKS, LANES), F32),
                        pltpu.VMEM((tm * ROW_CHUNKS, LANES), F32),
                        pltpu.SemaphoreType.DMA((2,)),
                        pltpu.SemaphoreType.DMA((2,))],
    )
    return pl.pallas_call(
        kern,
        out_shape=jax.ShapeDtypeStruct(((n_rows + tm) * ROW_CHUNKS, LANES), F32),
        grid_spec=grid_spec,
        compiler_params=_cparams(("arbitrary",)),
        name="moe_experts",
    )(plan, tok, dst, xg, wgu, bgu, wd, bd)


def _route(topi, t, tm):
    n_rows = TOP_K * t
    e_flat = topi[:, :TOP_K].T.reshape(-1)
    order = jnp.argsort(e_flat, stable=True).astype(jnp.int32)
    experts = jnp.arange(N_EXPERTS, dtype=jnp.int32)
    counts = jnp.sum((e_flat[:, None] == experts[None, :]).astype(jnp.int32), axis=0)
    off = jnp.cumsum(counts) - counts
    tiles = (counts + tm - 1) // tm
    tile_end = jnp.cumsum(tiles)
    tile_off = tile_end - tiles
    total = tile_end[-1]
    n_tiles = n_rows // tm + N_EXPERTS
    g = jnp.arange(n_tiles, dtype=jnp.int32)
    te = jnp.minimum(jnp.searchsorted(tile_end, g, side="right"), N_EXPERTS - 1).astype(jnp.int32)
    r = jnp.arange(tm, dtype=jnp.int32)
    start = (g - tile_off[te]) * tm
    nv = jnp.where(g < total, jnp.clip(counts[te] - start, 0, tm), 0).astype(jnp.int32)
    local = start[:, None] + r[None, :]
    valid = r[None, :] < nv[:, None]
    src = jnp.clip(off[te][:, None] + local, 0, n_rows - 1)
    f = order[src]
    tok = (jnp.where(valid, f % t, 0) * ROW_CHUNKS).astype(jnp.int32)
    dst = (jnp.where(valid, f, n_rows + r[None, :]) * ROW_CHUNKS).astype(jnp.int32)

    def lagged(a, lag):
        return jnp.pad(a, ((lag, _MOE_LAG - lag),) + ((0, 0),) * (a.ndim - 1))

    plan = jnp.stack([jnp.pad(te, (1, _MOE_LAG - 1), mode="edge"),
                      lagged(nv, 0), lagged(nv, 1), lagged(nv, 2), lagged(nv, 3)])
    n_steps = n_tiles + _MOE_LAG
    return (plan, lagged(tok, 0).reshape(n_steps, 1, tm), lagged(dst, 2).reshape(n_steps, 1, tm))


def _combine_kernel(xn_ref, tw_ref, y0_ref, y1_ref, y2_ref, y3_ref, lng_ref, lnb_ref, o_ref):
    tw = tw_ref[...]
    f = None
    for k, y_ref in enumerate((y0_ref, y1_ref, y2_ref, y3_ref)):
        term = tw[:, k:k + 1] * _load_row_tiles(y_ref, tw.shape[0])
        f = term if f is None else f + term
    o_ref[...] = _layer_norm(DEEPNORM_ALPHA * xn_ref[...] + f, lng_ref[...], lnb_ref[...])


def _combine(xn, tw, yb, ln_g, ln_b, tm):
    t = xn.shape[0]
    nblk = t // tm
    ysp = lambda k: pl.BlockSpec((tm * ROW_CHUNKS, LANES), lambda i, k=k: (k * nblk + i, 0))
    return pl.pallas_call(
        _combine_kernel,
        out_shape=jax.ShapeDtypeStruct((t, D_MODEL), F32),
        grid=(nblk,),
        in_specs=[pl.BlockSpec((tm, D_MODEL), lambda i: (i, 0)),
                  pl.BlockSpec((tm, LANES), lambda i: (i, 0)),
                  ysp(0), ysp(1), ysp(2), ysp(3),
                  pl.BlockSpec((1, D_MODEL), lambda i: (0, 0)),
                  pl.BlockSpec((1, D_MODEL), lambda i: (0, 0))],
        out_specs=pl.BlockSpec((tm, D_MODEL), lambda i: (i, 0)),
        compiler_params=_cparams(("parallel",)),
        name="combine_ln",
    )(xn, tw, yb, yb, yb, yb, ln_g, ln_b)


def kernel(x, mem, w_in, b_gate, diff_lambda, diff_subln_g, rel_bias, w_mem_kv, w_branch, w_out,
           ln1_g, ln1_b, router_w, router_b, w_gate_up, b_gate_up, w_down, b_down, ln2_g, ln2_b):
    b, s, d = x.shape
    t = b * s
    m = mem.shape[1]
    tq = min(ATT_TILE, s)
    tok_tile = min(TOK_TILE, t)

    idx = jnp.arange(tq, dtype=jnp.int32)
    ut = (idx[None, :] >= idx[:, None]).astype(BF16)
    bias = _bias_tiles(rel_bias, tq)
    mem2 = mem.reshape(b * m, d)
    xt = x.reshape(t, d)

    for l in range(DEPTH):
        w_in_l = jnp.concatenate([w_in[l][:, 7 * BRANCH_WIDTH:], w_in[l][:, :7 * BRANCH_WIDTH]],
                                 axis=1).astype(BF16)
        proj = _matmul(xt, w_in_l, min(1024, t), 512, "in_proj")
        proj3 = proj.reshape(b, s, IN_WIDTH)
        mem_kv = _matmul(mem2, w_mem_kv[l].astype(BF16), min(1024, b * m), 512, "mem_kv")
        mem_kv = mem_kv.reshape(b, m, 2 * BRANCH_WIDTH)

        lam_init = 0.8 - 0.6 * math.exp(-0.3 * l)
        sbv_t = _key_major(proj3[:, :, _SBV_BLK * LANES:_SBV_BLK * LANES + BRANCH_WIDTH], tq)
        dfv_t = _key_major(proj3[:, :, _DFV_BLK * LANES:_DFV_BLK * LANES + BRANCH_WIDTH], tq)
        ysb = _sb_attention(proj3, sbv_t, ut, tq)
        ydf = _df_attention(proj3, dfv_t, diff_lambda[l].astype(F32),
                            diff_subln_g[l].reshape(1, 2 * DIFF_HEAD_DIM).astype(F32),
                            bias, tq, lam_init)
        ymem = _mem_attention(proj3, mem_kv, min(512, s))

        rw = router_w[l].astype(F32)
        rwh = rw.astype(BF16)
        rwl = (rw - rwh.astype(F32)).astype(BF16)
        xn, xg, tw, ti = _merge(
            xt, proj, ysb.reshape(t, BRANCH_WIDTH), ydf.reshape(t, BRANCH_WIDTH),
            ymem.reshape(t, BRANCH_WIDTH), b_gate[l].reshape(N_BRANCHES, D_MODEL).astype(F32),
            w_branch[l].astype(BF16), w_out[l].astype(BF16),
            ln1_g[l].reshape(1, d).astype(F32), ln1_b[l].reshape(1, d).astype(F32),
            rwh, rwl, router_b[l].reshape(1, N_EXPERTS).astype(F32), tok_tile)

        plan, tok, dst = _route(ti, t, MOE_TILE)
        yb = _moe(xg, plan, tok, dst, w_gate_up[l].astype(BF16),
                  b_gate_up[l].reshape(N_EXPERTS, 1, 2 * D_FF).astype(F32),
                  w_down[l].astype(BF16), b_down[l].reshape(N_EXPERTS, 1, D_MODEL).astype(F32),
                  MOE_TILE)
        xt = _combine(xn, tw, yb, ln2_g[l].reshape(1, d).astype(F32),
                      ln2_b[l].reshape(1, d).astype(F32), tok_tile)
    return xt.reshape(b, s, d)
```

```python
import functools
import math

import jax
import jax.numpy as jnp
from jax import lax
from jax.experimental import pallas as pl
from jax.experimental.pallas import tpu as pltpu

F32 = jnp.float32
BF16 = jnp.bfloat16

D_MODEL = 1024
DEPTH = 2
CHUNK = 64
BRANCH_WIDTH = D_MODEL // 2
SB_HEAD_DIM = 64
DIFF_HEAD_DIM = 64
DIFF_HEADS = BRANCH_WIDTH // (2 * DIFF_HEAD_DIM)
MEM_HEAD_DIM = 128
MEM_HEADS = BRANCH_WIDTH // MEM_HEAD_DIM
N_BRANCHES = 3
GATE_WIDTH = N_BRANCHES * D_MODEL
IN_WIDTH = 7 * BRANCH_WIDTH + GATE_WIDTH
NUM_BUCKETS = 32
MAX_DISTANCE = 128
N_EXPERTS = 32
TOP_K = 4
D_FF = D_MODEL
SWIGLU_LIMIT = 7.0
SWIGLU_ALPHA = 1.702
LN_EPS = 1e-5
RMS_EPS = 1e-5
DEEPNORM_ALPHA = (2 * DEPTH) ** 0.25
LOG2E = math.log2(math.e)

LANES = 128
SUBLANES = 8
ROW_CHUNKS = D_MODEL // LANES
assert ROW_CHUNKS == SUBLANES
NEG_BIG = -1e30

_GATE_BLK = 0
_SBQ_BLK = GATE_WIDTH // LANES
_SBK_BLK = _SBQ_BLK + 4
_SBV_BLK = _SBQ_BLK + 8
_DFQ_BLK = _SBQ_BLK + 12
_DFK_BLK = _SBQ_BLK + 16
_DFV_BLK = _SBQ_BLK + 20
_MEMQ_BLK = _SBQ_BLK + 24

VMEM_LIMIT = 56 * 1024 * 1024

ATT_TILE = 256
MOE_TILE = 256
TOK_TILE = 512


def _cparams(sem):
    return pltpu.CompilerParams(dimension_semantics=sem, vmem_limit_bytes=VMEM_LIMIT)


def _load_row_tiles(ref, n):
    return jnp.concatenate([ref[pl.ds(c, n, stride=ROW_CHUNKS), :] for c in range(ROW_CHUNKS)],
                           axis=1)


def _store_row_tiles(ref, value):
    n = value.shape[0]
    for c in range(ROW_CHUNKS):
        ref[pl.ds(c, n, stride=ROW_CHUNKS), :] = value[:, c * LANES:(c + 1) * LANES]


def _matmul_kernel(a_ref, w_ref, o_ref):
    a = a_ref[...].astype(BF16)
    o_ref[...] = jnp.dot(a, w_ref[...], preferred_element_type=F32).astype(o_ref.dtype)


def _matmul(a, w, tm, tn, name):
    m, k = a.shape
    n = w.shape[1]
    return pl.pallas_call(
        _matmul_kernel,
        out_shape=jax.ShapeDtypeStruct((m, n), BF16),
        grid=(m // tm, n // tn),
        in_specs=[pl.BlockSpec((tm, k), lambda i, j: (i, 0)),
                  pl.BlockSpec((k, tn), lambda i, j: (0, j))],
        out_specs=pl.BlockSpec((tm, tn), lambda i, j: (i, j)),
        compiler_params=_cparams(("parallel", "arbitrary")),
        name=name,
    )(a, w)


_NT = (((1,), (1,)), ((), ()))


def _sb_kernel(q_ref, k_ref, vt_ref, ut_ref, o_ref, c_ref, acc_ref, zn_ref, a_ref, *, tq):
    i = pl.program_id(1)
    n_heads = 2 * (BRANCH_WIDTH // LANES)
    low = lax.broadcasted_iota(jnp.int32, (tq, LANES), 1) < SB_HEAD_DIM
    q_heads = []
    for hp in range(n_heads // 2):
        qs = q_ref[0, :, hp * LANES:(hp + 1) * LANES].astype(F32) * (-LOG2E * SB_HEAD_DIM ** -0.5)
        q_heads += [jnp.where(low, qs, 0.0).astype(BF16), jnp.where(low, 0.0, qs).astype(BF16)]
    ut = ut_ref[...]
    key = lax.broadcasted_iota(jnp.int32, (tq, tq), 0)
    qry = lax.broadcasted_iota(jnp.int32, (tq, tq), 1)
    earlier = key < qry

    c_ref[...] = jnp.zeros_like(c_ref)
    acc_ref[...] = jnp.zeros_like(acc_ref)

    def scores(j):
        start = pl.multiple_of(j * tq, tq)
        for h in range(n_heads):
            kb = k_ref[0, pl.ds(start, tq), (h // 2) * LANES:(h // 2 + 1) * LANES]
            zn_ref[h] = lax.dot_general(kb, q_heads[h], _NT, preferred_element_type=F32)

    def weights(masked):
        l1ms = []
        for h in range(n_heads):
            zn = zn_ref[h]
            neg_abs = lax.bitcast_convert_type(
                lax.bitcast_convert_type(zn, jnp.uint32) | jnp.uint32(0x80000000), F32)
            l1m = jnp.minimum(zn, 0.0) - jnp.log2(1.0 + jnp.exp2(neg_abs))
            if masked:
                l1m = jnp.where(earlier, l1m, 0.0)
            l1ms.append(l1m.astype(BF16))
        incls = [jnp.dot(ut, l1ms[h], preferred_element_type=F32) for h in range(n_heads)]
        for h in range(n_heads):
            c = c_ref[h:h + 1, :]
            a = jnp.exp2(incls[h] + c - zn_ref[h])
            if masked:
                a = jnp.where(earlier, a, 0.0)
            a_ref[h] = a.astype(BF16)
            c_ref[h:h + 1, :] = c + incls[h][0:1, :]

    def values(j):
        for h in range(n_heads):
            vtb = vt_ref[0, j, (h // 2) * LANES:(h // 2 + 1) * LANES, :]
            acc_ref[h] += jnp.dot(vtb, a_ref[h], preferred_element_type=F32)

    scores(i)
    weights(True)
    scores(jnp.maximum(i - 1, 0))

    def body(n, carry):
        j = i - 1 - n
        values(j + 1)
        weights(False)
        scores(jnp.maximum(j - 1, 0))
        return carry

    lax.fori_loop(0, i, body, 0)
    values(0)
    sub = lax.broadcasted_iota(jnp.int32, (LANES, tq), 0)
    for hp in range(n_heads // 2):
        ot = jnp.where(sub < SB_HEAD_DIM, acc_ref[2 * hp], acc_ref[2 * hp + 1])
        o_ref[0, :, hp * LANES:(hp + 1) * LANES] = ot.T.astype(o_ref.dtype)


def _sb_attention(proj, vt, ut, tq):
    b, s, _ = proj.shape
    n_heads = 2 * (BRANCH_WIDTH // LANES)
    wblk = BRANCH_WIDTH // LANES
    kern = functools.partial(_sb_kernel, tq=tq)
    return pl.pallas_call(
        kern,
        out_shape=jax.ShapeDtypeStruct((b, s, BRANCH_WIDTH), BF16),
        grid=(b, s // tq),
        in_specs=[pl.BlockSpec((1, tq, BRANCH_WIDTH), lambda bi, i: (bi, i, _SBQ_BLK // wblk)),
                  pl.BlockSpec((1, s, BRANCH_WIDTH), lambda bi, i: (bi, 0, _SBK_BLK // wblk)),
                  pl.BlockSpec((1, s // tq, BRANCH_WIDTH, tq), lambda bi, i: (bi, 0, 0, 0)),
                  pl.BlockSpec((tq, tq), lambda bi, i: (0, 0))],
        out_specs=pl.BlockSpec((1, tq, BRANCH_WIDTH), lambda bi, i: (bi, i, 0)),
        scratch_shapes=[pltpu.VMEM((n_heads, tq), F32),
                        pltpu.VMEM((n_heads, LANES, tq), F32),
                        pltpu.VMEM((n_heads, tq, tq), F32),
                        pltpu.VMEM((n_heads, tq, tq), BF16)],
        compiler_params=_cparams(("parallel", "arbitrary")),
        name="sb_attention",
    )(proj, proj, vt, ut)


def _df_kernel(lam_ref, g_ref, q_ref, k_ref, vt_ref, bias_ref, o_ref, mx_ref, den_ref, acc_ref,
               sc_ref, p_ref, *, tq, lam_init):
    i = pl.program_id(1)
    n_maps = 2 * DIFF_HEADS
    low = lax.broadcasted_iota(jnp.int32, (tq, LANES), 1) < DIFF_HEAD_DIM
    q_maps = []
    for h in range(DIFF_HEADS):
        qs = q_ref[0, :, h * LANES:(h + 1) * LANES].astype(F32) * (LOG2E * DIFF_HEAD_DIM ** -0.5)
        q_maps += [jnp.where(low, qs, 0.0).astype(BF16), jnp.where(low, 0.0, qs).astype(BF16)]

    mx_ref[...] = jnp.full_like(mx_ref, NEG_BIG)
    den_ref[...] = jnp.zeros_like(den_ref)
    acc_ref[...] = jnp.zeros_like(acc_ref)

    def scores(j):
        start = pl.multiple_of(j * tq, tq)
        for m in range(n_maps):
            kb = k_ref[0, pl.ds(start, tq), (m // 2) * LANES:(m // 2 + 1) * LANES]
            sc_ref[m] = lax.dot_general(kb, q_maps[m], _NT, preferred_element_type=F32)

    def values(j):
        return [jnp.dot(vt_ref[0, j, (m // 2) * LANES:(m // 2 + 1) * LANES, :], p_ref[m],
                        preferred_element_type=F32) for m in range(n_maps)]

    def softmax(pvs, kind, off=None):
        for m in range(n_maps):
            sc = sc_ref[m]
            if kind is not None:
                sc = sc + bias_ref[m, kind]
            if off is not None:
                sc = sc + off
            mx = mx_ref[m:m + 1, :]
            mx_new = jnp.maximum(mx, jnp.max(sc, axis=0, keepdims=True))
            alpha = jnp.exp2(mx - mx_new)
            p = jnp.exp2(sc - mx_new)
            den_ref[m:m + 1, :] = alpha * den_ref[m:m + 1, :] + jnp.sum(p, axis=0, keepdims=True)
            mx_ref[m:m + 1, :] = mx_new
            p_ref[m] = p.astype(BF16)
            if pvs is not None:
                acc_ref[m] = alpha * (acc_ref[m] + pvs[m])

    scores(i)
    softmax(None, 0)
    pvs = values(i)
    scores(jnp.maximum(i - 1, 0))
    softmax(pvs, 1, jnp.where(i == 0, NEG_BIG, 0.0).astype(F32))
    scores(jnp.maximum(i - 2, 0))

    def body(n, carry):
        j = i - 2 - n
        pvs = values(j + 1)
        softmax(pvs, None)
        scores(jnp.maximum(j - 1, 0))
        return carry

    lax.fori_loop(0, jnp.maximum(i - 1, 0), body, 0)
    pvs = values(0)
    for m in range(n_maps):
        acc_ref[m] += pvs[m]


    lf = lam_ref[...]
    lam = (jnp.exp(jnp.sum(lf[0:1] * lf[1:2], axis=-1, keepdims=True))
           - jnp.exp(jnp.sum(lf[2:3] * lf[3:4], axis=-1, keepdims=True)) + lam_init)
    for h in range(DIFF_HEADS):
        m0, m1 = 2 * h, 2 * h + 1
        o = (acc_ref[m0] / den_ref[m0:m0 + 1, :]
             - lam * (acc_ref[m1] / den_ref[m1:m1 + 1, :]))
        o = o * lax.rsqrt(jnp.mean(o * o, axis=0, keepdims=True) + RMS_EPS)
        o_ref[0, :, h * LANES:(h + 1) * LANES] = (
            o.T * g_ref[...] * (1.0 - lam_init)).astype(o_ref.dtype)


def _df_attention(proj, vt, lam_rows, subln_g, bias, tq, lam_init):
    b, s, _ = proj.shape
    n_maps = 2 * DIFF_HEADS
    wblk = BRANCH_WIDTH // LANES
    kern = functools.partial(_df_kernel, tq=tq, lam_init=lam_init)
    return pl.pallas_call(
        kern,
        out_shape=jax.ShapeDtypeStruct((b, s, BRANCH_WIDTH), BF16),
        grid=(b, s // tq),
        in_specs=[pl.BlockSpec((4, DIFF_HEAD_DIM), lambda bi, i: (0, 0)),
                  pl.BlockSpec((1, LANES), lambda bi, i: (0, 0)),
                  pl.BlockSpec((1, tq, BRANCH_WIDTH), lambda bi, i: (bi, i, _DFQ_BLK // wblk)),
                  pl.BlockSpec((1, s, BRANCH_WIDTH), lambda bi, i: (bi, 0, _DFK_BLK // wblk)),
                  pl.BlockSpec((1, s // tq, BRANCH_WIDTH, tq), lambda bi, i: (bi, 0, 0, 0)),
                  pl.BlockSpec((n_maps, 2, tq, tq), lambda bi, i: (0, 0, 0, 0))],
        out_specs=pl.BlockSpec((1, tq, BRANCH_WIDTH), lambda bi, i: (bi, i, 0)),
        scratch_shapes=[pltpu.VMEM((n_maps, tq), F32),
                        pltpu.VMEM((n_maps, tq), F32),
                        pltpu.VMEM((n_maps, LANES, tq), F32),
                        pltpu.VMEM((n_maps, tq, tq), F32),
                        pltpu.VMEM((n_maps, tq, tq), BF16)],
        compiler_params=_cparams(("parallel", "arbitrary")),
        name="diff_attention",
    )(lam_rows, subln_g, proj, proj, vt, bias)


def _key_major(v, tk):
    b, s, w = v.shape
    return v.reshape(b, s // tk, tk, w).swapaxes(2, 3)


def _t5_bucket(rel):
    half = NUM_BUCKETS // 2
    max_exact = half // 2
    n = jnp.abs(rel)
    nf = jnp.maximum(n, 1).astype(F32)
    large = max_exact + (jnp.log(nf / max_exact) / math.log(MAX_DISTANCE / max_exact)
                         * (half - max_exact)).astype(jnp.int32)
    large = jnp.minimum(large, half - 1)
    return jnp.where(rel > 0, half, 0) + jnp.where(n < max_exact, n, large)


def _bias_tiles(rel_bias, tq):
    table = rel_bias.astype(F32)
    r = jnp.arange(tq, dtype=jnp.int32)
    rel_diag = r[None, :] - r[:, None]
    far = table[_t5_bucket(jnp.full((), -(tq + 1), jnp.int32))]
    diag = (table[_t5_bucket(rel_diag)] - far) * LOG2E
    visible = (r[None, :] // CHUNK) <= (r[:, None] // CHUNK)
    diag = jnp.where(visible[..., None], diag, NEG_BIG)
    near = (table[_t5_bucket(rel_diag - tq)] - far) * LOG2E
    return jnp.stack([diag, near], axis=0).transpose(3, 0, 2, 1)


def _mem_kernel(q_ref, kv_ref, o_ref):
    for h in range(MEM_HEADS):
        lo = h * MEM_HEAD_DIM
        q = (q_ref[0, :, lo:lo + MEM_HEAD_DIM].astype(F32) * (MEM_HEAD_DIM ** -0.5)).astype(BF16)
        k = kv_ref[0, :, lo:lo + MEM_HEAD_DIM]
        v = kv_ref[0, :, BRANCH_WIDTH + lo:BRANCH_WIDTH + lo + MEM_HEAD_DIM]
        sc = lax.dot_general(q, k, (((1,), (1,)), ((), ())), preferred_element_type=F32)
        p = jnp.exp(sc - jnp.max(sc, axis=-1, keepdims=True))
        den = jnp.sum(p, axis=-1, keepdims=True)
        o = jnp.dot(p.astype(BF16), v, preferred_element_type=F32) / den
        o_ref[0, :, lo:lo + MEM_HEAD_DIM] = o.astype(o_ref.dtype)


def _mem_attention(proj, mem_kv, tq):
    b, s, _ = proj.shape
    m = mem_kv.shape[1]
    return pl.pallas_call(
        _mem_kernel,
        out_shape=jax.ShapeDtypeStruct((b, s, BRANCH_WIDTH), BF16),
        grid=(b, s // tq),
        in_specs=[pl.BlockSpec((1, tq, BRANCH_WIDTH),
                               lambda bi, i: (bi, i, _MEMQ_BLK * LANES // BRANCH_WIDTH)),
                  pl.BlockSpec((1, m, 2 * BRANCH_WIDTH), lambda bi, i: (bi, 0, 0))],
        out_specs=pl.BlockSpec((1, tq, BRANCH_WIDTH), lambda bi, i: (bi, i, 0)),
        compiler_params=_cparams(("parallel", "arbitrary")),
        name="mem_attention",
    )(proj, mem_kv)


def _layer_norm(r, g, b):
    mu = jnp.mean(r, axis=-1, keepdims=True)
    rc = r - mu
    var = jnp.mean(rc * rc, axis=-1, keepdims=True)
    return rc * lax.rsqrt(var + LN_EPS) * g + b


def _merge_kernel(x_ref, g0_ref, g1_ref, g2_ref, ysb_ref, ydf_ref, ymem_ref, bg_ref, wb_ref,
                  wo_ref, lng_ref, lnb_ref, rwh_ref, rwl_ref, rb_ref, xn_ref, xg_ref, tw_ref, ti_ref):
    merged = None
    for br, (g_ref, y_ref) in enumerate(((g0_ref, ysb_ref), (g1_ref, ydf_ref), (g2_ref, ymem_ref))):
        gate = jax.nn.sigmoid(g_ref[...].astype(F32) + bg_ref[br:br + 1, :])
        term = gate * jnp.dot(y_ref[...], wb_ref[br], preferred_element_type=F32)
        merged = term if merged is None else merged + term
    h = jnp.dot(merged.astype(BF16), wo_ref[...], preferred_element_type=F32)
    xn = _layer_norm(DEEPNORM_ALPHA * x_ref[...].astype(F32) + h, lng_ref[...], lnb_ref[...])
    xn_ref[...] = xn
    _store_row_tiles(xg_ref, xn)

    xh = xn.astype(BF16)
    xl = (xn - xh.astype(F32)).astype(BF16)
    logits = (jnp.dot(xh, rwh_ref[...], preferred_element_type=F32)
              + jnp.dot(xl, rwh_ref[...], preferred_element_type=F32)
              + jnp.dot(xh, rwl_ref[...], preferred_element_type=F32)) + rb_ref[...]
    tm = logits.shape[0]
    eid = lax.broadcasted_iota(jnp.int32, (tm, N_EXPERTS), 1).astype(F32)
    cur = logits
    vals, idxs = [], []
    for _ in range(TOP_K):
        best = jnp.max(cur, axis=-1, keepdims=True)
        idx = jnp.min(jnp.where(cur == best, eid, float(N_EXPERTS)), axis=-1, keepdims=True)
        vals.append(best)
        idxs.append(idx)
        cur = jnp.where(eid == idx, -jnp.inf, cur)
    exps = [jnp.exp(v - vals[0]) for v in vals]
    den = exps[0] + exps[1] + exps[2] + exps[3]
    lane = lax.broadcasted_iota(jnp.int32, (tm, LANES), 1)
    tw = jnp.zeros((tm, LANES), F32)
    ti = jnp.zeros((tm, LANES), F32)
    for k in range(TOP_K):
        tw = jnp.where(lane == k, exps[k] / den, tw)
        ti = jnp.where(lane == k, idxs[k], ti)
    tw_ref[...] = tw
    ti_ref[...] = ti.astype(jnp.int32)


def _merge(x, proj, ysb, ydf, ymem, b_gate, wb, wo, ln_g, ln_b, rwh, rwl, rb, tm):
    t = x.shape[0]
    tok = lambda blk: pl.BlockSpec((tm, blk[0]), lambda i, c=blk[1]: (i, c))
    full2 = lambda shape: pl.BlockSpec(shape, lambda i: (0, 0))
    return pl.pallas_call(
        _merge_kernel,
        out_shape=(jax.ShapeDtypeStruct((t, D_MODEL), F32),
                   jax.ShapeDtypeStruct((t * ROW_CHUNKS, LANES), F32),
                   jax.ShapeDtypeStruct((t, LANES), F32),
                   jax.ShapeDtypeStruct((t, LANES), jnp.int32)),
        grid=(t // tm,),
        in_specs=[tok((D_MODEL, 0)),
                  tok((D_MODEL, 0)), tok((D_MODEL, 1)), tok((D_MODEL, 2)),
                  tok((BRANCH_WIDTH, 0)), tok((BRANCH_WIDTH, 0)), tok((BRANCH_WIDTH, 0)),
                  full2((N_BRANCHES, D_MODEL)),
                  pl.BlockSpec((N_BRANCHES, BRANCH_WIDTH, D_MODEL), lambda i: (0, 0, 0)),
                  full2((D_MODEL, D_MODEL)),
                  full2((1, D_MODEL)), full2((1, D_MODEL)),
                  full2((D_MODEL, N_EXPERTS)), full2((D_MODEL, N_EXPERTS)),
                  full2((1, N_EXPERTS))],
        out_specs=(pl.BlockSpec((tm, D_MODEL), lambda i: (i, 0)),
                   pl.BlockSpec((tm * ROW_CHUNKS, LANES), lambda i: (i, 0)),
                   pl.BlockSpec((tm, LANES), lambda i: (i, 0)),
                   pl.BlockSpec((tm, LANES), lambda i: (i, 0))),
        compiler_params=_cparams(("parallel",)),
        name="merge_ln_router",
    )(x, proj, proj, proj, ysb, ydf, ymem, b_gate, wb, wo, ln_g, ln_b, rwh, rwl, rb)


_MOE_LAG = 3
_MOE_CHUNKS = 4


def _moe_kernel(plan_ref, tok_ref, dst_ref, x_hbm, wgu_ref, bgu_ref, wd_ref, bd_ref,
                y_hbm, xbuf0, xbuf1, ybuf0, ybuf1, xb_ref, yacc_ref, sem_in, sem_out, *, tm, n_rows):
    s = pl.program_id(0)
    do_gather = plan_ref[1, s] > 0
    do_compute = plan_ref[2, s] > 0
    do_scatter = plan_ref[3, s] > 0
    do_drain = plan_ref[4, s] > 0
    steady = do_gather & do_compute & do_scatter & do_drain
    parity = s & 1

    @pl.when(s == 0)
    def _():
        ybuf1[...] = jnp.zeros_like(ybuf1)
        init = pltpu.make_async_copy(ybuf1, y_hbm.at[pl.ds(n_rows * ROW_CHUNKS, tm * ROW_CHUNKS), :],
                                     sem_out.at[1])
        init.start()
        init.wait()

    def stages(par):
        x_new, y_old = (xbuf0, ybuf0) if par == 0 else (xbuf1, ybuf1)
        x_cur, y_cur = (xbuf1, ybuf1) if par == 0 else (xbuf0, ybuf0)
        new, cur = par, 1 - par

        def gather_row(r):
            t = pl.multiple_of(tok_ref[0, 0, r], ROW_CHUNKS)
            pltpu.make_async_copy(x_hbm.at[pl.ds(t, ROW_CHUNKS), :],
                                  x_new.at[pl.ds(r * ROW_CHUNKS, ROW_CHUNKS), :],
                                  sem_in.at[new]).start(priority=0)

        def scatter_row(r):
            d = pl.multiple_of(dst_ref[0, 0, r], ROW_CHUNKS)
            pltpu.make_async_copy(y_old.at[pl.ds(r * ROW_CHUNKS, ROW_CHUNKS), :],
                                  y_hbm.at[pl.ds(d, ROW_CHUNKS), :], sem_out.at[new]).start(priority=1)

        def wait_gather():
            pltpu.make_async_copy(x_hbm.at[pl.ds(0, tm * ROW_CHUNKS), :], x_cur, sem_in.at[cur]).wait()

        def wait_scatter():
            pltpu.make_async_copy(y_cur, y_hbm.at[pl.ds(0, tm * ROW_CHUNKS), :], sem_out.at[cur]).wait()

        def compute():
            xb = _load_row_tiles(x_cur, tm).astype(BF16)
            h = jnp.dot(xb, wgu_ref[0], preferred_element_type=F32) + bgu_ref[0]
            gate = jnp.minimum(h[:, :D_FF], SWIGLU_LIMIT)
            up = jnp.clip(h[:, D_FF:], -SWIGLU_LIMIT, SWIGLU_LIMIT)
            act = (up + 1.0) * (gate * jax.nn.sigmoid(SWIGLU_ALPHA * gate))
            _store_row_tiles(
                y_cur, jnp.dot(act.astype(BF16), wd_ref[0], preferred_element_type=F32) + bd_ref[0])

        def compute_chunk(c):
            fc = D_FF // _MOE_CHUNKS
            lo = c * fc
            if c == 0:
                xb_ref[...] = _load_row_tiles(x_cur, tm).astype(BF16)
            xb = xb_ref[...]
            gate = (jnp.dot(xb, wgu_ref[0, :, lo:lo + fc], preferred_element_type=F32)
                    + bgu_ref[0, :, lo:lo + fc])
            up = (jnp.dot(xb, wgu_ref[0, :, D_FF + lo:D_FF + lo + fc], preferred_element_type=F32)
                  + bgu_ref[0, :, D_FF + lo:D_FF + lo + fc])
            gate = jnp.minimum(gate, SWIGLU_LIMIT)
            up = jnp.clip(up, -SWIGLU_LIMIT, SWIGLU_LIMIT)
            act = (up + 1.0) * (gate * jax.nn.sigmoid(SWIGLU_ALPHA * gate))
            part = jnp.dot(act.astype(BF16), wd_ref[0, lo:lo + fc, :], preferred_element_type=F32)
            if c == 0:
                yacc_ref[...] = part + bd_ref[0]
            elif c < _MOE_CHUNKS - 1:
                yacc_ref[...] += part
            else:
                _store_row_tiles(y_cur, yacc_ref[...] + part)

        return gather_row, scatter_row, wait_gather, wait_scatter, compute, compute_chunk

    for par in range(2):
        gather_row, scatter_row, wait_gather, wait_scatter, compute, compute_chunk = stages(par)
        fast = steady & (parity == par)
        rows = tm // _MOE_CHUNKS

        for c in range(_MOE_CHUNKS):
            @pl.when(fast & (plan_ref[1, s] > -c))
            def _(c=c):
                if c == 0:
                    wait_gather()
                    wait_scatter()
                compute_chunk(c)
                for r in range(c * rows, (c + 1) * rows):
                    gather_row(r)
                    scatter_row(r)

        @pl.when(jnp.logical_not(steady) & (parity == par))
        def _():
            pl.when(do_compute)(wait_gather)
            pl.when(do_drain)(wait_scatter)

            @pl.when(do_gather)
            def _():
                lax.fori_loop(0, tm, lambda r, c: (gather_row(r), c)[1], 0)

            @pl.when(do_scatter)
            def _():
                lax.fori_loop(0, tm, lambda r, c: (scatter_row(r), c)[1], 0)

            pl.when(do_compute)(compute)


def _moe(xg, plan, tok, dst, wgu, bgu, wd, bd, tm):
    t = xg.shape[0] // ROW_CHUNKS
    n_steps = tok.shape[0]
    n_rows = TOP_K * t
    kern = functools.partial(_moe_kernel, tm=tm, n_rows=n_rows)
    grid_spec = pltpu.PrefetchScalarGridSpec(
        num_scalar_prefetch=1,
        grid=(n_steps,),
        in_specs=[pl.BlockSpec((1, 1, tm), lambda s, plan: (s, 0, 0), memory_space=pltpu.SMEM),
                  pl.BlockSpec((1, 1, tm), lambda s, plan: (s, 0, 0), memory_space=pltpu.SMEM),
                  pl.BlockSpec(memory_space=pl.ANY),
                  pl.BlockSpec((1, D_MODEL, 2 * D_FF), lambda s, plan: (plan[0, s], 0, 0)),
                  pl.BlockSpec((1, 1, 2 * D_FF), lambda s, plan: (plan[0, s], 0, 0)),
                  pl.BlockSpec((1, D_FF, D_MODEL), lambda s, plan: (plan[0, s], 0, 0)),
                  pl.BlockSpec((1, 1, D_MODEL), lambda s, plan: (plan[0, s], 0, 0))],
        out_specs=pl.BlockSpec(memory_space=pl.ANY),
        scratch_shapes=[pltpu.VMEM((tm * ROW_CHUNKS, LANES), F32),
                        pltpu.VMEM((tm * ROW_CHUNKS, LANES), F32),
                        pltpu.VMEM((tm * ROW_CHUNKS, LANES), F32),
                        pltpu.VMEM((tm * ROW_CHUNKS, LANES), F32),
                        pltpu.VMEM((tm, D_MODEL), BF16),
                        pltpu.VMEM((tm, D_MODEL), F32),
                        pltpu.SemaphoreType.DMA((2,)),
                        pltpu.SemaphoreType.DMA((2,))],
    )
    return pl.pallas_call(
        kern,
        out_shape=jax.ShapeDtypeStruct(((n_rows + tm) * ROW_CHUNKS, LANES), F32),
        grid_spec=grid_spec,
        compiler_params=_cparams(("arbitrary",)),
        name="moe_experts",
    )(plan, tok, dst, xg, wgu, bgu, wd, bd)


def _route(topi, t, tm):
    n_rows = TOP_K * t
    e_flat = topi[:, :TOP_K].T.reshape(-1)
    order = jnp.argsort(e_flat, stable=True).astype(jnp.int32)
    experts = jnp.arange(N_EXPERTS, dtype=jnp.int32)
    counts = jnp.sum((e_flat[:, None] == experts[None, :]).astype(jnp.int32), axis=0)
    off = jnp.cumsum(counts) - counts
    tiles = (counts + tm - 1) // tm
    tile_end = jnp.cumsum(tiles)
    tile_off = tile_end - tiles
    total = tile_end[-1]
    n_tiles = n_rows // tm + N_EXPERTS
    g = jnp.arange(n_tiles, dtype=jnp.int32)
    te = jnp.minimum(jnp.searchsorted(tile_end, g, side="right"), N_EXPERTS - 1).astype(jnp.int32)
    r = jnp.arange(tm, dtype=jnp.int32)
    start = (g - tile_off[te]) * tm
    nv = jnp.where(g < total, jnp.clip(counts[te] - start, 0, tm), 0).astype(jnp.int32)
    local = start[:, None] + r[None, :]
    valid = r[None, :] < nv[:, None]
    src = jnp.clip(off[te][:, None] + local, 0, n_rows - 1)
    f = order[src]
    tok = (jnp.where(valid, f % t, 0) * ROW_CHUNKS).astype(jnp.int32)
    dst = (jnp.where(valid, f, n_rows + r[None, :]) * ROW_CHUNKS).astype(jnp.int32)

    def lagged(a, lag):
        return jnp.pad(a, ((lag, _MOE_LAG - lag),) + ((0, 0),) * (a.ndim - 1))

    plan = jnp.stack([jnp.pad(te, (1, _MOE_LAG - 1), mode="edge"),
                      lagged(nv, 0), lagged(nv, 1), lagged(nv, 2), lagged(nv, 3)])
    n_steps = n_tiles + _MOE_LAG
    return (plan, lagged(tok, 0).reshape(n_steps, 1, tm), lagged(dst, 2).reshape(n_steps, 1, tm))


def _combine_kernel(xn_ref, tw_ref, y0_ref, y1_ref, y2_ref, y3_ref, lng_ref, lnb_ref, o_ref):
    tw = tw_ref[...]
    f = None
    for k, y_ref in enumerate((y0_ref, y1_ref, y2_ref, y3_ref)):
        term = tw[:, k:k + 1] * _load_row_tiles(y_ref, tw.shape[0])
        f = term if f is None else f + term
    o_ref[...] = _layer_norm(DEEPNORM_ALPHA * xn_ref[...] + f, lng_ref[...], lnb_ref[...])


def _combine(xn, tw, yb, ln_g, ln_b, tm):
    t = xn.shape[0]
    nblk = t // tm
    ysp = lambda k: pl.BlockSpec((tm * ROW_CHUNKS, LANES), lambda i, k=k: (k * nblk + i, 0))
    return pl.pallas_call(
        _combine_kernel,
        out_shape=jax.ShapeDtypeStruct((t, D_MODEL), F32),
        grid=(nblk,),
        in_specs=[pl.BlockSpec((tm, D_MODEL), lambda i: (i, 0)),
                  pl.BlockSpec((tm, LANES), lambda i: (i, 0)),
                  ysp(0), ysp(1), ysp(2), ysp(3),
                  pl.BlockSpec((1, D_MODEL), lambda i: (0, 0)),
                  pl.BlockSpec((1, D_MODEL), lambda i: (0, 0))],
        out_specs=pl.BlockSpec((tm, D_MODEL), lambda i: (i, 0)),
        compiler_params=_cparams(("parallel",)),
        name="combine_ln",
    )(xn, tw, yb, yb, yb, yb, ln_g, ln_b)


def kernel(x, mem, w_in, b_gate, diff_lambda, diff_subln_g, rel_bias, w_mem_kv, w_branch, w_out,
           ln1_g, ln1_b, router_w, router_b, w_gate_up, b_gate_up, w_down, b_down, ln2_g, ln2_b):
    b, s, d = x.shape
    t = b * s
    m = mem.shape[1]
    tq = min(ATT_TILE, s)
    tok_tile = min(TOK_TILE, t)

    idx = jnp.arange(tq, dtype=jnp.int32)
    ut = (idx[None, :] >= idx[:, None]).astype(BF16)
    bias = _bias_tiles(rel_bias, tq)
    mem2 = mem.reshape(b * m, d)
    xt = x.reshape(t, d)

    for l in range(DEPTH):
        w_in_l = jnp.concatenate([w_in[l][:, 7 * BRANCH_WIDTH:], w_in[l][:, :7 * BRANCH_WIDTH]],
                                 axis=1).astype(BF16)
        proj = _matmul(xt, w_in_l, min(1024, t), 512, "in_proj")
        proj3 = proj.reshape(b, s, IN_WIDTH)
        mem_kv = _matmul(mem2, w_mem_kv[l].astype(BF16), min(1024, b * m), 512, "mem_kv")
        mem_kv = mem_kv.reshape(b, m, 2 * BRANCH_WIDTH)

        lam_init = 0.8 - 0.6 * math.exp(-0.3 * l)
        sbv_t = _key_major(proj3[:, :, _SBV_BLK * LANES:_SBV_BLK * LANES + BRANCH_WIDTH], tq)
        dfv_t = _key_major(proj3[:, :, _DFV_BLK * LANES:_DFV_BLK * LANES + BRANCH_WIDTH], tq)
        ysb = _sb_attention(proj3, sbv_t, ut, tq)
        ydf = _df_attention(proj3, dfv_t, diff_lambda[l].astype(F32),
                            diff_subln_g[l].reshape(1, 2 * DIFF_HEAD_DIM).astype(F32),
                            bias, tq, lam_init)
        ymem = _mem_attention(proj3, mem_kv, min(512, s))

        rw = router_w[l].astype(F32)
        rwh = rw.astype(BF16)
        rwl = (rw - rwh.astype(F32)).astype(BF16)
        xn, xg, tw, ti = _merge(
            xt, proj, ysb.reshape(t, BRANCH_WIDTH), ydf.reshape(t, BRANCH_WIDTH),
            ymem.reshape(t, BRANCH_WIDTH), b_gate[l].reshape(N_BRANCHES, D_MODEL).astype(F32),
            w_branch[l].astype(BF16), w_out[l].astype(BF16),
            ln1_g[l].reshape(1, d).astype(F32), ln1_b[l].reshape(1, d).astype(F32),
            rwh, rwl, router_b[l].reshape(1, N_EXPERTS).astype(F32), tok_tile)

        plan, tok, dst = _route(ti, t, MOE_TILE)
        yb = _moe(xg, plan, tok, dst, w_gate_up[l].astype(BF16),
                  b_gate_up[l].reshape(N_EXPERTS, 1, 2 * D_FF).astype(F32),
                  w_down[l].astype(BF16), b_down[l].reshape(N_EXPERTS, 1, D_MODEL).astype(F32),
                  MOE_TILE)
        xt = _combine(xn, tw, yb, ln2_g[l].reshape(1, d).astype(F32),
                      ln2_b[l].reshape(1, d).astype(F32), tok_tile)
    return xt.reshape(b, s, d)
```

```python
import functools
import math

import jax
import jax.numpy as jnp
from jax import lax
from jax.experimental import pallas as pl
from jax.experimental.pallas import tpu as pltpu

F32 = jnp.float32
BF16 = jnp.bfloat16

D_MODEL = 1024
DEPTH = 2
CHUNK = 64
BRANCH_WIDTH = D_MODEL // 2
SB_HEAD_DIM = 64
DIFF_HEAD_DIM = 64
DIFF_HEADS = BRANCH_WIDTH // (2 * DIFF_HEAD_DIM)
MEM_HEAD_DIM = 128
MEM_HEADS = BRANCH_WIDTH // MEM_HEAD_DIM
N_BRANCHES = 3
GATE_WIDTH = N_BRANCHES * D_MODEL
IN_WIDTH = 7 * BRANCH_WIDTH + GATE_WIDTH
NUM_BUCKETS = 32
MAX_DISTANCE = 128
N_EXPERTS = 32
TOP_K = 4
D_FF = D_MODEL
SWIGLU_LIMIT = 7.0
SWIGLU_ALPHA = 1.702
LN_EPS = 1e-5
RMS_EPS = 1e-5
DEEPNORM_ALPHA = (2 * DEPTH) ** 0.25
LOG2E = math.log2(math.e)

LANES = 128
SUBLANES = 8
ROW_CHUNKS = D_MODEL // LANES
assert ROW_CHUNKS == SUBLANES
NEG_BIG = -1e30

_GATE_BLK = 0
_SBQ_BLK = GATE_WIDTH // LANES
_SBK_BLK = _SBQ_BLK + 4
_SBV_BLK = _SBQ_BLK + 8
_DFQ_BLK = _SBQ_BLK + 12
_DFK_BLK = _SBQ_BLK + 16
_DFV_BLK = _SBQ_BLK + 20
_MEMQ_BLK = _SBQ_BLK + 24

VMEM_LIMIT = 56 * 1024 * 1024

ATT_TILE = 256
MOE_TILE = 256
TOK_TILE = 512


def _cparams(sem):
    return pltpu.CompilerParams(dimension_semantics=sem, vmem_limit_bytes=VMEM_LIMIT)


def _load_row_tiles(ref, n):
    return jnp.concatenate([ref[pl.ds(c, n, stride=ROW_CHUNKS), :] for c in range(ROW_CHUNKS)],
                           axis=1)


def _store_row_tiles(ref, value):
    n = value.shape[0]
    for c in range(ROW_CHUNKS):
        ref[pl.ds(c, n, stride=ROW_CHUNKS), :] = value[:, c * LANES:(c + 1) * LANES]


def _matmul_kernel(a_ref, w_ref, o_ref):
    a = a_ref[...].astype(BF16)
    o_ref[...] = jnp.dot(a, w_ref[...], preferred_element_type=F32).astype(o_ref.dtype)


def _matmul(a, w, tm, tn, name):
    m, k = a.shape
    n = w.shape[1]
    return pl.pallas_call(
        _matmul_kernel,
        out_shape=jax.ShapeDtypeStruct((m, n), BF16),
        grid=(m // tm, n // tn),
        in_specs=[pl.BlockSpec((tm, k), lambda i, j: (i, 0)),
                  pl.BlockSpec((k, tn), lambda i, j: (0, j))],
        out_specs=pl.BlockSpec((tm, tn), lambda i, j: (i, j)),
        compiler_params=_cparams(("parallel", "arbitrary")),
        name=name,
    )(a, w)


_NT = (((1,), (1,)), ((), ()))


def _sb_kernel(q_ref, k_ref, vt_ref, ut_ref, o_ref, c_ref, acc_ref, zn_ref, a_ref, *, tq):
    i = pl.program_id(1)
    n_heads = 2 * (BRANCH_WIDTH // LANES)
    low = lax.broadcasted_iota(jnp.int32, (tq, LANES), 1) < SB_HEAD_DIM
    q_heads = []
    for hp in range(n_heads // 2):
        qs = q_ref[0, :, hp * LANES:(hp + 1) * LANES].astype(F32) * (-LOG2E * SB_HEAD_DIM ** -0.5)
        q_heads += [jnp.where(low, qs, 0.0).astype(BF16), jnp.where(low, 0.0, qs).astype(BF16)]
    ut = ut_ref[...]
    key = lax.broadcasted_iota(jnp.int32, (tq, tq), 0)
    qry = lax.broadcasted_iota(jnp.int32, (tq, tq), 1)
    earlier = key < qry

    c_ref[...] = jnp.zeros_like(c_ref)
    acc_ref[...] = jnp.zeros_like(acc_ref)

    def scores(j):
        start = pl.multiple_of(j * tq, tq)
        for h in range(n_heads):
            kb = k_ref[0, pl.ds(start, tq), (h // 2) * LANES:(h // 2 + 1) * LANES]
            zn_ref[h] = lax.dot_general(kb, q_heads[h], _NT, preferred_element_type=F32)

    def weights(masked):
        l1ms = []
        for h in range(n_heads):
            zn = zn_ref[h]
            neg_abs = lax.bitcast_convert_type(
                lax.bitcast_convert_type(zn, jnp.uint32) | jnp.uint32(0x80000000), F32)
            l1m = jnp.minimum(zn, 0.0) - jnp.log2(1.0 + jnp.exp2(neg_abs))
            if masked:
                l1m = jnp.where(earlier, l1m, 0.0)
            l1ms.append(l1m.astype(BF16))
        incls = [jnp.dot(ut, l1ms[h], preferred_element_type=F32) for h in range(n_heads)]
        for h in range(n_heads):
            c = c_ref[h:h + 1, :]
            a = jnp.exp2(incls[h] + c - zn_ref[h])
            if masked:
                a = jnp.where(earlier, a, 0.0)
            a_ref[h] = a.astype(BF16)
            c_ref[h:h + 1, :] = c + incls[h][0:1, :]

    def values(j):
        for h in range(n_heads):
            vtb = vt_ref[0, j, (h // 2) * LANES:(h // 2 + 1) * LANES, :]
            acc_ref[h] += jnp.dot(vtb, a_ref[h], preferred_element_type=F32)

    scores(i)
    weights(True)
    scores(jnp.maximum(i - 1, 0))

    def body(n, carry):
        j = i - 1 - n
        values(j + 1)
        weights(False)
        scores(jnp.maximum(j - 1, 0))
        return carry

    lax.fori_loop(0, i, body, 0)
    values(0)
    sub = lax.broadcasted_iota(jnp.int32, (LANES, tq), 0)
    for hp in range(n_heads // 2):
        ot = jnp.where(sub < SB_HEAD_DIM, acc_ref[2 * hp], acc_ref[2 * hp + 1])
        o_ref[0, :, hp * LANES:(hp + 1) * LANES] = ot.T.astype(o_ref.dtype)


def _sb_attention(proj, vt, ut, tq):
    b, s, _ = proj.shape
    n_heads = 2 * (BRANCH_WIDTH // LANES)
    wblk = BRANCH_WIDTH // LANES
    kern = functools.partial(_sb_kernel, tq=tq)
    return pl.pallas_call(
        kern,
        out_shape=jax.ShapeDtypeStruct((b, s, BRANCH_WIDTH), BF16),
        grid=(b, s // tq),
        in_specs=[pl.BlockSpec((1, tq, BRANCH_WIDTH), lambda bi, i: (bi, i, _SBQ_BLK // wblk)),
                  pl.BlockSpec((1, s, BRANCH_WIDTH), lambda bi, i: (bi, 0, _SBK_BLK // wblk)),
                  pl.BlockSpec((1, s // tq, BRANCH_WIDTH, tq), lambda bi, i: (bi, 0, 0, 0)),
                  pl.BlockSpec((tq, tq), lambda bi, i: (0, 0))],
        out_specs=pl.BlockSpec((1, tq, BRANCH_WIDTH), lambda bi, i: (bi, i, 0)),
        scratch_shapes=[pltpu.VMEM((n_heads, tq), F32),
                        pltpu.VMEM((n_heads, LANES, tq), F32),
                        pltpu.VMEM((n_heads, tq, tq), F32),
                        pltpu.VMEM((n_heads, tq, tq), BF16)],
        compiler_params=_cparams(("parallel", "arbitrary")),
        name="sb_attention",
    )(proj, proj, vt, ut)


def _df_kernel(lam_ref, g_ref, q_ref, k_ref, vt_ref, bias_ref, o_ref, mx_ref, den_ref, acc_ref,
               sc_ref, p_ref, *, tq, lam_init):
    i = pl.program_id(1)
    n_maps = 2 * DIFF_HEADS
    low = lax.broadcasted_iota(jnp.int32, (tq, LANES), 1) < DIFF_HEAD_DIM
    q_maps = []
    for h in range(DIFF_HEADS):
        qs = q_ref[0, :, h * LANES:(h + 1) * LANES].astype(F32) * (LOG2E * DIFF_HEAD_DIM ** -0.5)
        q_maps += [jnp.where(low, qs, 0.0).astype(BF16), jnp.where(low, 0.0, qs).astype(BF16)]

    mx_ref[...] = jnp.full_like(mx_ref, NEG_BIG)
    den_ref[...] = jnp.zeros_like(den_ref)
    acc_ref[...] = jnp.zeros_like(acc_ref)

    def scores(j):
        start = pl.multiple_of(j * tq, tq)
        for m in range(n_maps):
            kb = k_ref[0, pl.ds(start, tq), (m // 2) * LANES:(m // 2 + 1) * LANES]
            sc_ref[m] = lax.dot_general(kb, q_maps[m], _NT, preferred_element_type=F32)

    def values(j):
        return [jnp.dot(vt_ref[0, j, (m // 2) * LANES:(m // 2 + 1) * LANES, :], p_ref[m],
                        preferred_element_type=F32) for m in range(n_maps)]

    def softmax(pvs, kind, off=None):
        for m in range(n_maps):
            sc = sc_ref[m]
            if kind is not None:
                sc = sc + bias_ref[m, kind]
            if off is not None:
                sc = sc + off
            mx = mx_ref[m:m + 1, :]
            mx_new = jnp.maximum(mx, jnp.max(sc, axis=0, keepdims=True))
            alpha = jnp.exp2(mx - mx_new)
            p = jnp.exp2(sc - mx_new)
            den_ref[m:m + 1, :] = alpha * den_ref[m:m + 1, :] + jnp.sum(p, axis=0, keepdims=True)
            mx_ref[m:m + 1, :] = mx_new
            p_ref[m] = p.astype(BF16)
            if pvs is not None:
                acc_ref[m] = alpha * (acc_ref[m] + pvs[m])

    scores(i)
    softmax(None, 0)
    pvs = values(i)
    scores(jnp.maximum(i - 1, 0))
    softmax(pvs, 1, jnp.where(i == 0, NEG_BIG, 0.0).astype(F32))
    scores(jnp.maximum(i - 2, 0))

    def body(n, carry):
        j = i - 2 - n
        pvs = values(j + 1)
        softmax(pvs, None)
        scores(jnp.maximum(j - 1, 0))
        return carry

    lax.fori_loop(0, jnp.maximum(i - 1, 0), body, 0)
    pvs = values(0)
    for m in range(n_maps):
        acc_ref[m] += pvs[m]


    lf = lam_ref[...]
    lam = (jnp.exp(jnp.sum(lf[0:1] * lf[1:2], axis=-1, keepdims=True))
           - jnp.exp(jnp.sum(lf[2:3] * lf[3:4], axis=-1, keepdims=True)) + lam_init)
    for h in range(DIFF_HEADS):
        m0, m1 = 2 * h, 2 * h + 1
        o = (acc_ref[m0] / den_ref[m0:m0 + 1, :]
             - lam * (acc_ref[m1] / den_ref[m1:m1 + 1, :]))
        o = o * lax.rsqrt(jnp.mean(o * o, axis=0, keepdims=True) + RMS_EPS)
        o_ref[0, :, h * LANES:(h + 1) * LANES] = (
            o.T * g_ref[...] * (1.0 - lam_init)).astype(o_ref.dtype)


def _df_attention(proj, vt, lam_rows, subln_g, bias, tq, lam_init):
    b, s, _ = proj.shape
    n_maps = 2 * DIFF_HEADS
    wblk = BRANCH_WIDTH // LANES
    kern = functools.partial(_df_kernel, tq=tq, lam_init=lam_init)
    return pl.pallas_call(
        kern,
        out_shape=jax.ShapeDtypeStruct((b, s, BRANCH_WIDTH), BF16),
        grid=(b, s // tq),
        in_specs=[pl.BlockSpec((4, DIFF_HEAD_DIM), lambda bi, i: (0, 0)),
                  pl.BlockSpec((1, LANES), lambda bi, i: (0, 0)),
                  pl.BlockSpec((1, tq, BRANCH_WIDTH), lambda bi, i: (bi, i, _DFQ_BLK // wblk)),
                  pl.BlockSpec((1, s, BRANCH_WIDTH), lambda bi, i: (bi, 0, _DFK_BLK // wblk)),
                  pl.BlockSpec((1, s // tq, BRANCH_WIDTH, tq), lambda bi, i: (bi, 0, 0, 0)),
                  pl.BlockSpec((n_maps, 2, tq, tq), lambda bi, i: (0, 0, 0, 0))],
        out_specs=pl.BlockSpec((1, tq, BRANCH_WIDTH), lambda bi, i: (bi, i, 0)),
        scratch_shapes=[pltpu.VMEM((n_maps, tq), F32),
                        pltpu.VMEM((n_maps, tq), F32),
                        pltpu.VMEM((n_maps, LANES, tq), F32),
                        pltpu.VMEM((n_maps, tq, tq), F32),
                        pltpu.VMEM((n_maps, tq, tq), BF16)],
        compiler_params=_cparams(("parallel", "arbitrary")),
        name="diff_attention",
    )(lam_rows, subln_g, proj, proj, vt, bias)


def _key_major(v, tk):
    b, s, w = v.shape
    return v.reshape(b, s // tk, tk, w).swapaxes(2, 3)


def _t5_bucket(rel):
    half = NUM_BUCKETS // 2
    max_exact = half // 2
    n = jnp.abs(rel)
    nf = jnp.maximum(n, 1).astype(F32)
    large = max_exact + (jnp.log(nf / max_exact) / math.log(MAX_DISTANCE / max_exact)
                         * (half - max_exact)).astype(jnp.int32)
    large = jnp.minimum(large, half - 1)
    return jnp.where(rel > 0, half, 0) + jnp.where(n < max_exact, n, large)


def _bias_tiles(rel_bias, tq):
    table = rel_bias.astype(F32)
    r = jnp.arange(tq, dtype=jnp.int32)
    rel_diag = r[None, :] - r[:, None]
    far = table[_t5_bucket(jnp.full((), -(tq + 1), jnp.int32))]
    diag = (table[_t5_bucket(rel_diag)] - far) * LOG2E
    visible = (r[None, :] // CHUNK) <= (r[:, None] // CHUNK)
    diag = jnp.where(visible[..., None], diag, NEG_BIG)
    near = (table[_t5_bucket(rel_diag - tq)] - far) * LOG2E
    return jnp.stack([diag, near], axis=0).transpose(3, 0, 2, 1)


def _mem_kernel(q_ref, kv_ref, o_ref):
    for h in range(MEM_HEADS):
        lo = h * MEM_HEAD_DIM
        q = (q_ref[0, :, lo:lo + MEM_HEAD_DIM].astype(F32) * (MEM_HEAD_DIM ** -0.5)).astype(BF16)
        k = kv_ref[0, :, lo:lo + MEM_HEAD_DIM]
        v = kv_ref[0, :, BRANCH_WIDTH + lo:BRANCH_WIDTH + lo + MEM_HEAD_DIM]
        sc = lax.dot_general(q, k, (((1,), (1,)), ((), ())), preferred_element_type=F32)
        p = jnp.exp(sc - jnp.max(sc, axis=-1, keepdims=True))
        den = jnp.sum(p, axis=-1, keepdims=True)
        o = jnp.dot(p.astype(BF16), v, preferred_element_type=F32) / den
        o_ref[0, :, lo:lo + MEM_HEAD_DIM] = o.astype(o_ref.dtype)


def _mem_attention(proj, mem_kv, tq):
    b, s, _ = proj.shape
    m = mem_kv.shape[1]
    return pl.pallas_call(
        _mem_kernel,
        out_shape=jax.ShapeDtypeStruct((b, s, BRANCH_WIDTH), BF16),
        grid=(b, s // tq),
        in_specs=[pl.BlockSpec((1, tq, BRANCH_WIDTH),
                               lambda bi, i: (bi, i, _MEMQ_BLK * LANES // BRANCH_WIDTH)),
                  pl.BlockSpec((1, m, 2 * BRANCH_WIDTH), lambda bi, i: (bi, 0, 0))],
        out_specs=pl.BlockSpec((1, tq, BRANCH_WIDTH), lambda bi, i: (bi, i, 0)),
        compiler_params=_cparams(("parallel", "arbitrary")),
        name="mem_attention",
    )(proj, mem_kv)


def _layer_norm(r, g, b):
    mu = jnp.mean(r, axis=-1, keepdims=True)
    rc = r - mu
    var = jnp.mean(rc * rc, axis=-1, keepdims=True)
    return rc * lax.rsqrt(var + LN_EPS) * g + b


def _merge_kernel(x_ref, g0_ref, g1_ref, g2_ref, ysb_ref, ydf_ref, ymem_ref, bg_ref, wb_ref,
                  wo_ref, lng_ref, lnb_ref, rwh_ref, rwl_ref, rb_ref, xn_ref, xg_ref, tw_ref, ti_ref):
    merged = None
    for br, (g_ref, y_ref) in enumerate(((g0_ref, ysb_ref), (g1_ref, ydf_ref), (g2_ref, ymem_ref))):
        gate = jax.nn.sigmoid(g_ref[...].astype(F32) + bg_ref[br:br + 1, :])
        term = gate * jnp.dot(y_ref[...], wb_ref[br], preferred_element_type=F32)
        merged = term if merged is None else merged + term
    h = jnp.dot(merged.astype(BF16), wo_ref[...], preferred_element_type=F32)
    xn = _layer_norm(DEEPNORM_ALPHA * x_ref[...].astype(F32) + h, lng_ref[...], lnb_ref[...])
    xn_ref[...] = xn
    _store_row_tiles(xg_ref, xn)

    xh = xn.astype(BF16)
    xl = (xn - xh.astype(F32)).astype(BF16)
    logits = (jnp.dot(xh, rwh_ref[...], preferred_element_type=F32)
              + jnp.dot(xl, rwh_ref[...], preferred_element_type=F32)
              + jnp.dot(xh, rwl_ref[...], preferred_element_type=F32)) + rb_ref[...]
    tm = logits.shape[0]
    eid = lax.broadcasted_iota(jnp.int32, (tm, N_EXPERTS), 1).astype(F32)
    cur = logits
    vals, idxs = [], []
    for _ in range(TOP_K):
        best = jnp.max(cur, axis=-1, keepdims=True)
        idx = jnp.min(jnp.where(cur == best, eid, float(N_EXPERTS)), axis=-1, keepdims=True)
        vals.append(best)
        idxs.append(idx)
        cur = jnp.where(eid == idx, -jnp.inf, cur)
    exps = [jnp.exp(v - vals[0]) for v in vals]
    den = exps[0] + exps[1] + exps[2] + exps[3]
    lane = lax.broadcasted_iota(jnp.int32, (tm, LANES), 1)
    tw = jnp.zeros((tm, LANES), F32)
    ti = jnp.zeros((tm, LANES), F32)
    for k in range(TOP_K):
        tw = jnp.where(lane == k, exps[k] / den, tw)
        ti = jnp.where(lane == k, idxs[k], ti)
    tw_ref[...] = tw
    ti_ref[...] = ti.astype(jnp.int32)


def _merge(x, proj, ysb, ydf, ymem, b_gate, wb, wo, ln_g, ln_b, rwh, rwl, rb, tm):
    t = x.shape[0]
    tok = lambda blk: pl.BlockSpec((tm, blk[0]), lambda i, c=blk[1]: (i, c))
    full2 = lambda shape: pl.BlockSpec(shape, lambda i: (0, 0))
    return pl.pallas_call(
        _merge_kernel,
        out_shape=(jax.ShapeDtypeStruct((t, D_MODEL), F32),
                   jax.ShapeDtypeStruct((t * ROW_CHUNKS, LANES), F32),
                   jax.ShapeDtypeStruct((t, LANES), F32),
                   jax.ShapeDtypeStruct((t, LANES), jnp.int32)),
        grid=(t // tm,),
        in_specs=[tok((D_MODEL, 0)),
                  tok((D_MODEL, 0)), tok((D_MODEL, 1)), tok((D_MODEL, 2)),
                  tok((BRANCH_WIDTH, 0)), tok((BRANCH_WIDTH, 0)), tok((BRANCH_WIDTH, 0)),
                  full2((N_BRANCHES, D_MODEL)),
                  pl.BlockSpec((N_BRANCHES, BRANCH_WIDTH, D_MODEL), lambda i: (0, 0, 0)),
                  full2((D_MODEL, D_MODEL)),
                  full2((1, D_MODEL)), full2((1, D_MODEL)),
                  full2((D_MODEL, N_EXPERTS)), full2((D_MODEL, N_EXPERTS)),
                  full2((1, N_EXPERTS))],
        out_specs=(pl.BlockSpec((tm, D_MODEL), lambda i: (i, 0)),
                   pl.BlockSpec((tm * ROW_CHUNKS, LANES), lambda i: (i, 0)),
                   pl.BlockSpec((tm, LANES), lambda i: (i, 0)),
                   pl.BlockSpec((tm, LANES), lambda i: (i, 0))),
        compiler_params=_cparams(("parallel",)),
        name="merge_ln_router",
    )(x, proj, proj, proj, ysb, ydf, ymem, b_gate, wb, wo, ln_g, ln_b, rwh, rwl, rb)


_MOE_LAG = 3


def _moe_kernel(plan_ref, tok_ref, dst_ref, x_hbm, wgu_ref, bgu_ref, wd_ref, bd_ref,
                y_hbm, xbuf0, xbuf1, ybuf0, ybuf1, wgu_bf, wd_bf, sem_in, sem_out,
                *, tm, n_rows):
    s = pl.program_id(0)
    do_gather = plan_ref[1, s] > 0
    do_compute = plan_ref[2, s] > 0
    do_scatter = plan_ref[3, s] > 0
    do_drain = plan_ref[4, s] > 0
    steady = do_gather & do_compute & do_scatter & do_drain
    parity = s & 1

    @pl.when(plan_ref[5, s] > 0)
    def _():
        wgu_bf[...] = wgu_ref[0].astype(BF16)
        wd_bf[...] = wd_ref[0].astype(BF16)

    @pl.when(s == 0)
    def _():
        ybuf1[...] = jnp.zeros_like(ybuf1)
        init = pltpu.make_async_copy(ybuf1, y_hbm.at[pl.ds(n_rows * ROW_CHUNKS, tm * ROW_CHUNKS), :],
                                     sem_out.at[1])
        init.start()
        init.wait()

    def stages(par):
        x_new, y_old = (xbuf0, ybuf0) if par == 0 else (xbuf1, ybuf1)
        x_cur, y_cur = (xbuf1, ybuf1) if par == 0 else (xbuf0, ybuf0)
        new, cur = par, 1 - par

        def gather_row(r):
            t = pl.multiple_of(tok_ref[0, 0, r], ROW_CHUNKS)
            pltpu.make_async_copy(x_hbm.at[pl.ds(t, ROW_CHUNKS), :],
                                  x_new.at[pl.ds(r * ROW_CHUNKS, ROW_CHUNKS), :],
                                  sem_in.at[new]).start(priority=0)

        def scatter_row(r):
            d = pl.multiple_of(dst_ref[0, 0, r], ROW_CHUNKS)
            pltpu.make_async_copy(y_old.at[pl.ds(r * ROW_CHUNKS, ROW_CHUNKS), :],
                                  y_hbm.at[pl.ds(d, ROW_CHUNKS), :], sem_out.at[new]).start(priority=1)

        def wait_gather():
            pltpu.make_async_copy(x_hbm.at[pl.ds(0, tm * ROW_CHUNKS), :], x_cur, sem_in.at[cur]).wait()

        def wait_scatter():
            pltpu.make_async_copy(y_cur, y_hbm.at[pl.ds(0, tm * ROW_CHUNKS), :], sem_out.at[cur]).wait()

        def compute():
            xb = _load_row_tiles(x_cur, tm).astype(BF16)
            h = jnp.dot(xb, wgu_bf[...], preferred_element_type=F32) + bgu_ref[0]
            gate = jnp.minimum(h[:, :D_FF], SWIGLU_LIMIT)
            up = jnp.clip(h[:, D_FF:], -SWIGLU_LIMIT, SWIGLU_LIMIT)
            act = (up + 1.0) * (gate * jax.nn.sigmoid(SWIGLU_ALPHA * gate))
            _store_row_tiles(
                y_cur, jnp.dot(act.astype(BF16), wd_bf[...], preferred_element_type=F32) + bd_ref[0])

        return gather_row, scatter_row, wait_gather, wait_scatter, compute

    for par in range(2):
        gather_row, scatter_row, wait_gather, wait_scatter, compute = stages(par)
        fast = steady & (parity == par)

        @pl.when(fast)
        def _():
            for r in range(tm):
                gather_row(r)
                scatter_row(r)

        @pl.when(fast & (plan_ref[1, s] > -1))
        def _():
            wait_gather()
            wait_scatter()
            compute()

        @pl.when(jnp.logical_not(steady) & (parity == par))
        def _():
            pl.when(do_compute)(wait_gather)
            pl.when(do_drain)(wait_scatter)

            @pl.when(do_gather)
            def _():
                lax.fori_loop(0, tm, lambda r, c: (gather_row(r), c)[1], 0)

            @pl.when(do_scatter)
            def _():
                lax.fori_loop(0, tm, lambda r, c: (scatter_row(r), c)[1], 0)

            pl.when(do_compute)(compute)


def _moe(xg, plan, tok, dst, wgu, bgu, wd, bd, tm):
    t = xg.shape[0] // ROW_CHUNKS
    n_steps = tok.shape[0]
    n_rows = TOP_K * t
    kern = functools.partial(_moe_kernel, tm=tm, n_rows=n_rows)
    grid_spec = pltpu.PrefetchScalarGridSpec(
        num_scalar_prefetch=1,
        grid=(n_steps,),
        in_specs=[pl.BlockSpec((1, 1, tm), lambda s, plan: (s, 0, 0), memory_space=pltpu.SMEM),
                  pl.BlockSpec((1, 1, tm), lambda s, plan: (s, 0, 0), memory_space=pltpu.SMEM),
                  pl.BlockSpec(memory_space=pl.ANY),
                  pl.BlockSpec((1, D_MODEL, 2 * D_FF), lambda s, plan: (plan[0, s], 0, 0)),
                  pl.BlockSpec((1, 1, 2 * D_FF), lambda s, plan: (plan[0, s], 0, 0)),
                  pl.BlockSpec((1, D_FF, D_MODEL), lambda s, plan: (plan[0, s], 0, 0)),
                  pl.BlockSpec((1, 1, D_MODEL), lambda s, plan: (plan[0, s], 0, 0))],
        out_specs=pl.BlockSpec(memory_space=pl.ANY),
        scratch_shapes=[pltpu.VMEM((tm * ROW_CHUNKS, LANES), F32),
                        pltpu.VMEM((tm * ROW_CHUNKS, LANES), F32),
                        pltpu.VMEM((tm * ROW_CHUNKS, LANES), F32),
                        pltpu.VMEM((tm * ROW_CHUNKS, LANES), F32),
                        pltpu.VMEM((D_MODEL, 2 * D_FF), BF16),
                        pltpu.VMEM((D_FF, D_MODEL), BF16),
                        pltpu.SemaphoreType.DMA((2,)),
                        pltpu.SemaphoreType.DMA((2,))],
    )
    return pl.pallas_call(
        kern,
        out_shape=jax.ShapeDtypeStruct(((n_rows + tm) * ROW_CHUNKS, LANES), F32),
        grid_spec=grid_spec,
        compiler_params=_cparams(("arbitrary",)),
        name="moe_experts",
    )(plan, tok, dst, xg, wgu, bgu, wd, bd)


def _route(topi, t, tm):
    n_rows = TOP_K * t
    e_flat = topi[:, :TOP_K].T.reshape(-1)
    order = jnp.argsort(e_flat, stable=True).astype(jnp.int32)
    experts = jnp.arange(N_EXPERTS, dtype=jnp.int32)
    counts = jnp.sum((e_flat[:, None] == experts[None, :]).astype(jnp.int32), axis=0)
    off = jnp.cumsum(counts) - counts
    tiles = (counts + tm - 1) // tm
    tile_end = jnp.cumsum(tiles)
    tile_off = tile_end - tiles
    total = tile_end[-1]
    n_tiles = n_rows // tm + N_EXPERTS
    g = jnp.arange(n_tiles, dtype=jnp.int32)
    te = jnp.minimum(jnp.sum((g[:, None] >= tile_end[None, :]).astype(jnp.int32), axis=1),
                     N_EXPERTS - 1)
    r = jnp.arange(tm, dtype=jnp.int32)
    start = (g - tile_off[te]) * tm
    nv = jnp.where(g < total, jnp.clip(counts[te] - start, 0, tm), 0).astype(jnp.int32)
    local = start[:, None] + r[None, :]
    valid = r[None, :] < nv[:, None]
    first = jnp.minimum(off[te] + start, n_rows)
    order_tail = jnp.concatenate([order, jnp.zeros((tm,), jnp.int32)])
    f = jax.vmap(lambda a: lax.dynamic_slice(order_tail, (a,), (tm,)))(first)
    tok = (jnp.where(valid, f % t, 0) * ROW_CHUNKS).astype(jnp.int32)
    dst = (jnp.where(valid, f, n_rows + r[None, :]) * ROW_CHUNKS).astype(jnp.int32)

    def lagged(a, lag):
        return jnp.pad(a, ((lag, _MOE_LAG - lag),) + ((0, 0),) * (a.ndim - 1))

    te_step = jnp.pad(te, (1, _MOE_LAG - 1), mode="edge")
    fresh = jnp.concatenate([jnp.ones((2,), jnp.int32),
                             (te_step[2:] != te_step[1:-1]).astype(jnp.int32)])
    plan = jnp.stack([te_step, lagged(nv, 0), lagged(nv, 1), lagged(nv, 2), lagged(nv, 3), fresh])
    n_steps = n_tiles + _MOE_LAG
    return (plan, lagged(tok, 0).reshape(n_steps, 1, tm), lagged(dst, 2).reshape(n_steps, 1, tm))


def _combine_kernel(xn_ref, tw_ref, y0_ref, y1_ref, y2_ref, y3_ref, lng_ref, lnb_ref, o_ref):
    tw = tw_ref[...]
    f = None
    for k, y_ref in enumerate((y0_ref, y1_ref, y2_ref, y3_ref)):
        term = tw[:, k:k + 1] * _load_row_tiles(y_ref, tw.shape[0])
        f = term if f is None else f + term
    o_ref[...] = _layer_norm(DEEPNORM_ALPHA * xn_ref[...] + f, lng_ref[...], lnb_ref[...])


def _combine(xn, tw, yb, ln_g, ln_b, tm):
    t = xn.shape[0]
    nblk = t // tm
    ysp = lambda k: pl.BlockSpec((tm * ROW_CHUNKS, LANES), lambda i, k=k: (k * nblk + i, 0))
    return pl.pallas_call(
        _combine_kernel,
        out_shape=jax.ShapeDtypeStruct((t, D_MODEL), F32),
        grid=(nblk,),
        in_specs=[pl.BlockSpec((tm, D_MODEL), lambda i: (i, 0)),
                  pl.BlockSpec((tm, LANES), lambda i: (i, 0)),
                  ysp(0), ysp(1), ysp(2), ysp(3),
                  pl.BlockSpec((1, D_MODEL), lambda i: (0, 0)),
                  pl.BlockSpec((1, D_MODEL), lambda i: (0, 0))],
        out_specs=pl.BlockSpec((tm, D_MODEL), lambda i: (i, 0)),
        compiler_params=_cparams(("parallel",)),
        name="combine_ln",
    )(xn, tw, yb, yb, yb, yb, ln_g, ln_b)


def kernel(x, mem, w_in, b_gate, diff_lambda, diff_subln_g, rel_bias, w_mem_kv, w_branch, w_out,
           ln1_g, ln1_b, router_w, router_b, w_gate_up, b_gate_up, w_down, b_down, ln2_g, ln2_b):
    b, s, d = x.shape
    t = b * s
    m = mem.shape[1]
    tq = min(ATT_TILE, s)
    tok_tile = min(TOK_TILE, t)

    idx = jnp.arange(tq, dtype=jnp.int32)
    ut = (idx[None, :] >= idx[:, None]).astype(BF16)
    bias = _bias_tiles(rel_bias, tq)
    mem2 = mem.reshape(b * m, d)
    xt = x.reshape(t, d)

    for l in range(DEPTH):
        w_in_l = jnp.concatenate([w_in[l][:, 7 * BRANCH_WIDTH:], w_in[l][:, :7 * BRANCH_WIDTH]],
                                 axis=1).astype(BF16)
        proj = _matmul(xt, w_in_l, min(1024, t), IN_WIDTH // 2, "in_proj")
        proj3 = proj.reshape(b, s, IN_WIDTH)
        mem_kv = _matmul(mem2, w_mem_kv[l].astype(BF16), min(1024, b * m), 512, "mem_kv")
        mem_kv = mem_kv.reshape(b, m, 2 * BRANCH_WIDTH)

        lam_init = 0.8 - 0.6 * math.exp(-0.3 * l)
        sbv_t = _key_major(proj3[:, :, _SBV_BLK * LANES:_SBV_BLK * LANES + BRANCH_WIDTH], tq)
        dfv_t = _key_major(proj3[:, :, _DFV_BLK * LANES:_DFV_BLK * LANES + BRANCH_WIDTH], tq)
        ysb = _sb_attention(proj3, sbv_t, ut, tq)
        ydf = _df_attention(proj3, dfv_t, diff_lambda[l].astype(F32),
                            diff_subln_g[l].reshape(1, 2 * DIFF_HEAD_DIM).astype(F32),
                            bias, tq, lam_init)
        ymem = _mem_attention(proj3, mem_kv, min(512, s))

        rw = router_w[l].astype(F32)
        rwh = rw.astype(BF16)
        rwl = (rw - rwh.astype(F32)).astype(BF16)
        xn, xg, tw, ti = _merge(
            xt, proj, ysb.reshape(t, BRANCH_WIDTH), ydf.reshape(t, BRANCH_WIDTH),
            ymem.reshape(t, BRANCH_WIDTH), b_gate[l].reshape(N_BRANCHES, D_MODEL).astype(F32),
            w_branch[l].astype(BF16), w_out[l].astype(BF16),
            ln1_g[l].reshape(1, d).astype(F32), ln1_b[l].reshape(1, d).astype(F32),
            rwh, rwl, router_b[l].reshape(1, N_EXPERTS).astype(F32), tok_tile)

        plan, tok, dst = _route(ti, t, MOE_TILE)
        yb = _moe(xg, plan, tok, dst, w_gate_up[l],
                  b_gate_up[l].reshape(N_EXPERTS, 1, 2 * D_FF).astype(F32),
                  w_down[l], b_down[l].reshape(N_EXPERTS, 1, D_MODEL).astype(F32),
                  MOE_TILE)
        xt = _combine(xn, tw, yb, ln2_g[l].reshape(1, d).astype(F32),
                      ln2_b[l].reshape(1, d).astype(F32), tok_tile)
    return xt.reshape(b, s, d)
```

```python
import functools
import math

import jax
import jax.numpy as jnp
from jax import lax
from jax.experimental import pallas as pl
from jax.experimental.pallas import tpu as pltpu

F32 = jnp.float32
BF16 = jnp.bfloat16

D_MODEL = 1024
DEPTH = 2
CHUNK = 64
BRANCH_WIDTH = D_MODEL // 2
SB_HEAD_DIM = 64
DIFF_HEAD_DIM = 64
DIFF_HEADS = BRANCH_WIDTH // (2 * DIFF_HEAD_DIM)
MEM_HEAD_DIM = 128
MEM_HEADS = BRANCH_WIDTH // MEM_HEAD_DIM
N_BRANCHES = 3
GATE_WIDTH = N_BRANCHES * D_MODEL
IN_WIDTH = 7 * BRANCH_WIDTH + GATE_WIDTH
NUM_BUCKETS = 32
MAX_DISTANCE = 128
N_EXPERTS = 32
TOP_K = 4
D_FF = D_MODEL
SWIGLU_LIMIT = 7.0
SWIGLU_ALPHA = 1.702
LN_EPS = 1e-5
RMS_EPS = 1e-5
DEEPNORM_ALPHA = (2 * DEPTH) ** 0.25
LOG2E = math.log2(math.e)

LANES = 128
SUBLANES = 8
ROW_CHUNKS = D_MODEL // LANES
assert ROW_CHUNKS == SUBLANES
NEG_BIG = -1e30

_GATE_BLK = 0
_SBQ_BLK = GATE_WIDTH // LANES
_SBK_BLK = _SBQ_BLK + 4
_SBV_BLK = _SBQ_BLK + 8
_DFQ_BLK = _SBQ_BLK + 12
_DFK_BLK = _SBQ_BLK + 16
_DFV_BLK = _SBQ_BLK + 20
_MEMQ_BLK = _SBQ_BLK + 24

VMEM_LIMIT = 56 * 1024 * 1024

ATT_TILE = 256
MOE_TILE = 256
TOK_TILE = 512


def _cparams(sem):
    return pltpu.CompilerParams(dimension_semantics=sem, vmem_limit_bytes=VMEM_LIMIT)


def _load_row_tiles(ref, n):
    return jnp.concatenate([ref[pl.ds(c, n, stride=ROW_CHUNKS), :] for c in range(ROW_CHUNKS)],
                           axis=1)


def _store_row_tiles(ref, value):
    n = value.shape[0]
    for c in range(ROW_CHUNKS):
        ref[pl.ds(c, n, stride=ROW_CHUNKS), :] = value[:, c * LANES:(c + 1) * LANES]


def _matmul_kernel(a_ref, w_ref, o_ref):
    a = a_ref[...].astype(BF16)
    o_ref[...] = jnp.dot(a, w_ref[...], preferred_element_type=F32).astype(o_ref.dtype)


def _matmul(a, w, tm, tn, name):
    m, k = a.shape
    n = w.shape[1]
    return pl.pallas_call(
        _matmul_kernel,
        out_shape=jax.ShapeDtypeStruct((m, n), BF16),
        grid=(m // tm, n // tn),
        in_specs=[pl.BlockSpec((tm, k), lambda i, j: (i, 0)),
                  pl.BlockSpec((k, tn), lambda i, j: (0, j))],
        out_specs=pl.BlockSpec((tm, tn), lambda i, j: (i, j)),
        compiler_params=_cparams(("parallel", "arbitrary")),
        name=name,
    )(a, w)


_NT = (((1,), (1,)), ((), ()))


def _sb_kernel(q_ref, k_ref, vt_ref, ut_ref, o_ref, c_ref, acc_ref, zn_ref, a_ref, *, tq):
    i = pl.program_id(1)
    n_heads = 2 * (BRANCH_WIDTH // LANES)
    low = lax.broadcasted_iota(jnp.int32, (tq, LANES), 1) < SB_HEAD_DIM
    q_heads = []
    for hp in range(n_heads // 2):
        qs = q_ref[0, :, hp * LANES:(hp + 1) * LANES].astype(F32) * (-LOG2E * SB_HEAD_DIM ** -0.5)
        q_heads += [jnp.where(low, qs, 0.0).astype(BF16), jnp.where(low, 0.0, qs).astype(BF16)]
    ut = ut_ref[...]
    key = lax.broadcasted_iota(jnp.int32, (tq, tq), 0)
    qry = lax.broadcasted_iota(jnp.int32, (tq, tq), 1)
    earlier = key < qry

    c_ref[...] = jnp.zeros_like(c_ref)
    acc_ref[...] = jnp.zeros_like(acc_ref)

    def scores(j):
        start = pl.multiple_of(j * tq, tq)
        for h in range(n_heads):
            kb = k_ref[0, pl.ds(start, tq), (h // 2) * LANES:(h // 2 + 1) * LANES]
            zn_ref[h] = lax.dot_general(kb, q_heads[h], _NT, preferred_element_type=F32)

    def weights(masked):
        l1ms = []
        for h in range(n_heads):
            zn = zn_ref[h]
            neg_abs = lax.bitcast_convert_type(
                lax.bitcast_convert_type(zn, jnp.uint32) | jnp.uint32(0x80000000), F32)
            l1m = jnp.minimum(zn, 0.0) - jnp.log2(1.0 + jnp.exp2(neg_abs))
            if masked:
                l1m = jnp.where(earlier, l1m, 0.0)
            l1ms.append(l1m.astype(BF16))
        incls = [jnp.dot(ut, l1ms[h], preferred_element_type=F32) for h in range(n_heads)]
        for h in range(n_heads):
            c = c_ref[h:h + 1, :]
            a = jnp.exp2(incls[h] + c - zn_ref[h])
            if masked:
                a = jnp.where(earlier, a, 0.0)
            a_ref[h] = a.astype(BF16)
            c_ref[h:h + 1, :] = c + incls[h][0:1, :]

    def values(j):
        for h in range(n_heads):
            vtb = vt_ref[0, j, (h // 2) * LANES:(h // 2 + 1) * LANES, :]
            acc_ref[h] += jnp.dot(vtb, a_ref[h], preferred_element_type=F32)

    scores(i)
    weights(True)
    scores(jnp.maximum(i - 1, 0))

    def body(n, carry):
        j = i - 1 - n
        values(j + 1)
        weights(False)
        scores(jnp.maximum(j - 1, 0))
        return carry

    lax.fori_loop(0, i, body, 0)
    values(0)
    sub = lax.broadcasted_iota(jnp.int32, (LANES, tq), 0)
    for hp in range(n_heads // 2):
        ot = jnp.where(sub < SB_HEAD_DIM, acc_ref[2 * hp], acc_ref[2 * hp + 1])
        o_ref[0, :, hp * LANES:(hp + 1) * LANES] = ot.T.astype(o_ref.dtype)


def _sb_attention(proj, vt, ut, tq):
    b, s, _ = proj.shape
    n_heads = 2 * (BRANCH_WIDTH // LANES)
    wblk = BRANCH_WIDTH // LANES
    kern = functools.partial(_sb_kernel, tq=tq)
    return pl.pallas_call(
        kern,
        out_shape=jax.ShapeDtypeStruct((b, s, BRANCH_WIDTH), BF16),
        grid=(b, s // tq),
        in_specs=[pl.BlockSpec((1, tq, BRANCH_WIDTH), lambda bi, i: (bi, i, _SBQ_BLK // wblk)),
                  pl.BlockSpec((1, s, BRANCH_WIDTH), lambda bi, i: (bi, 0, _SBK_BLK // wblk)),
                  pl.BlockSpec((1, s // tq, BRANCH_WIDTH, tq), lambda bi, i: (bi, 0, 0, 0)),
                  pl.BlockSpec((tq, tq), lambda bi, i: (0, 0))],
        out_specs=pl.BlockSpec((1, tq, BRANCH_WIDTH), lambda bi, i: (bi, i, 0)),
        scratch_shapes=[pltpu.VMEM((n_heads, tq), F32),
                        pltpu.VMEM((n_heads, LANES, tq), F32),
                        pltpu.VMEM((n_heads, tq, tq), F32),
                        pltpu.VMEM((n_heads, tq, tq), BF16)],
        compiler_params=_cparams(("parallel", "arbitrary")),
        name="sb_attention",
    )(proj, proj, vt, ut)


def _df_kernel(lam_ref, g_ref, q_ref, k_ref, vt_ref, bias_ref, o_ref, mx_ref, den_ref, acc_ref,
               sc_ref, p_ref, *, tq, lam_init):
    i = pl.program_id(1)
    n_maps = 2 * DIFF_HEADS
    low = lax.broadcasted_iota(jnp.int32, (tq, LANES), 1) < DIFF_HEAD_DIM
    q_maps = []
    for h in range(DIFF_HEADS):
        qs = q_ref[0, :, h * LANES:(h + 1) * LANES].astype(F32) * (LOG2E * DIFF_HEAD_DIM ** -0.5)
        q_maps += [jnp.where(low, qs, 0.0).astype(BF16), jnp.where(low, 0.0, qs).astype(BF16)]

    mx_ref[...] = jnp.full_like(mx_ref, NEG_BIG)
    den_ref[...] = jnp.zeros_like(den_ref)
    acc_ref[...] = jnp.zeros_like(acc_ref)

    def scores(j):
        start = pl.multiple_of(j * tq, tq)
        for m in range(n_maps):
            kb = k_ref[0, pl.ds(start, tq), (m // 2) * LANES:(m // 2 + 1) * LANES]
            sc_ref[m] = lax.dot_general(kb, q_maps[m], _NT, preferred_element_type=F32)

    def values(j):
        return [jnp.dot(vt_ref[0, j, (m // 2) * LANES:(m // 2 + 1) * LANES, :], p_ref[m],
                        preferred_element_type=F32) for m in range(n_maps)]

    def softmax(pvs, kind, off=None):
        for m in range(n_maps):
            sc = sc_ref[m]
            if kind is not None:
                sc = sc + bias_ref[m, kind]
            if off is not None:
                sc = sc + off
            mx = mx_ref[m:m + 1, :]
            mx_new = jnp.maximum(mx, jnp.max(sc, axis=0, keepdims=True))
            alpha = jnp.exp2(mx - mx_new)
            p = jnp.exp2(sc - mx_new)
            den_ref[m:m + 1, :] = alpha * den_ref[m:m + 1, :] + jnp.sum(p, axis=0, keepdims=True)
            mx_ref[m:m + 1, :] = mx_new
            p_ref[m] = p.astype(BF16)
            if pvs is not None:
                acc_ref[m] = alpha * (acc_ref[m] + pvs[m])

    scores(i)
    softmax(None, 0)
    pvs = values(i)
    scores(jnp.maximum(i - 1, 0))
    softmax(pvs, 1, jnp.where(i == 0, NEG_BIG, 0.0).astype(F32))
    scores(jnp.maximum(i - 2, 0))

    def body(n, carry):
        j = i - 2 - n
        pvs = values(j + 1)
        softmax(pvs, None)
        scores(jnp.maximum(j - 1, 0))
        return carry

    lax.fori_loop(0, jnp.maximum(i - 1, 0), body, 0)
    pvs = values(0)
    for m in range(n_maps):
        acc_ref[m] += pvs[m]


    lf = lam_ref[...]
    lam = (jnp.exp(jnp.sum(lf[0:1] * lf[1:2], axis=-1, keepdims=True))
           - jnp.exp(jnp.sum(lf[2:3] * lf[3:4], axis=-1, keepdims=True)) + lam_init)
    for h in range(DIFF_HEADS):
        m0, m1 = 2 * h, 2 * h + 1
        o = (acc_ref[m0] / den_ref[m0:m0 + 1, :]
             - lam * (acc_ref[m1] / den_ref[m1:m1 + 1, :]))
        o = o * lax.rsqrt(jnp.mean(o * o, axis=0, keepdims=True) + RMS_EPS)
        o_ref[0, :, h * LANES:(h + 1) * LANES] = (
            o.T * g_ref[...] * (1.0 - lam_init)).astype(o_ref.dtype)


def _df_attention(proj, vt, lam_rows, subln_g, bias, tq, lam_init):
    b, s, _ = proj.shape
    n_maps = 2 * DIFF_HEADS
    wblk = BRANCH_WIDTH // LANES
    kern = functools.partial(_df_kernel, tq=tq, lam_init=lam_init)
    return pl.pallas_call(
        kern,
        out_shape=jax.ShapeDtypeStruct((b, s, BRANCH_WIDTH), BF16),
        grid=(b, s // tq),
        in_specs=[pl.BlockSpec((4, DIFF_HEAD_DIM), lambda bi, i: (0, 0)),
                  pl.BlockSpec((1, LANES), lambda bi, i: (0, 0)),
                  pl.BlockSpec((1, tq, BRANCH_WIDTH), lambda bi, i: (bi, i, _DFQ_BLK // wblk)),
                  pl.BlockSpec((1, s, BRANCH_WIDTH), lambda bi, i: (bi, 0, _DFK_BLK // wblk)),
                  pl.BlockSpec((1, s // tq, BRANCH_WIDTH, tq), lambda bi, i: (bi, 0, 0, 0)),
                  pl.BlockSpec((n_maps, 2, tq, tq), lambda bi, i: (0, 0, 0, 0))],
        out_specs=pl.BlockSpec((1, tq, BRANCH_WIDTH), lambda bi, i: (bi, i, 0)),
        scratch_shapes=[pltpu.VMEM((n_maps, tq), F32),
                        pltpu.VMEM((n_maps, tq), F32),
                        pltpu.VMEM((n_maps, LANES, tq), F32),
                        pltpu.VMEM((n_maps, tq, tq), F32),
                        pltpu.VMEM((n_maps, tq, tq), BF16)],
        compiler_params=_cparams(("parallel", "arbitrary")),
        name="diff_attention",
    )(lam_rows, subln_g, proj, proj, vt, bias)


def _key_major(v, tk):
    b, s, w = v.shape
    return v.reshape(b, s // tk, tk, w).swapaxes(2, 3)


def _t5_bucket(rel):
    half = NUM_BUCKETS // 2
    max_exact = half // 2
    n = jnp.abs(rel)
    nf = jnp.maximum(n, 1).astype(F32)
    large = max_exact + (jnp.log(nf / max_exact) / math.log(MAX_DISTANCE / max_exact)
                         * (half - max_exact)).astype(jnp.int32)
    large = jnp.minimum(large, half - 1)
    return jnp.where(rel > 0, half, 0) + jnp.where(n < max_exact, n, large)


def _bias_tiles(rel_bias, tq):
    table = rel_bias.astype(F32)
    r = jnp.arange(tq, dtype=jnp.int32)
    rel_diag = r[None, :] - r[:, None]
    far = table[_t5_bucket(jnp.full((), -(tq + 1), jnp.int32))]
    diag = (table[_t5_bucket(rel_diag)] - far) * LOG2E
    visible = (r[None, :] // CHUNK) <= (r[:, None] // CHUNK)
    diag = jnp.where(visible[..., None], diag, NEG_BIG)
    near = (table[_t5_bucket(rel_diag - tq)] - far) * LOG2E
    return jnp.stack([diag, near], axis=0).transpose(3, 0, 2, 1)


def _mem_kernel(q_ref, kv_ref, o_ref):
    for h in range(MEM_HEADS):
        lo = h * MEM_HEAD_DIM
        q = (q_ref[0, :, lo:lo + MEM_HEAD_DIM].astype(F32) * (MEM_HEAD_DIM ** -0.5)).astype(BF16)
        k = kv_ref[0, :, lo:lo + MEM_HEAD_DIM]
        v = kv_ref[0, :, BRANCH_WIDTH + lo:BRANCH_WIDTH + lo + MEM_HEAD_DIM]
        sc = lax.dot_general(q, k, (((1,), (1,)), ((), ())), preferred_element_type=F32)
        p = jnp.exp(sc - jnp.max(sc, axis=-1, keepdims=True))
        den = jnp.sum(p, axis=-1, keepdims=True)
        o = jnp.dot(p.astype(BF16), v, preferred_element_type=F32) / den
        o_ref[0, :, lo:lo + MEM_HEAD_DIM] = o.astype(o_ref.dtype)


def _mem_attention(proj, mem_kv, tq):
    b, s, _ = proj.shape
    m = mem_kv.shape[1]
    return pl.pallas_call(
        _mem_kernel,
        out_shape=jax.ShapeDtypeStruct((b, s, BRANCH_WIDTH), BF16),
        grid=(b, s // tq),
        in_specs=[pl.BlockSpec((1, tq, BRANCH_WIDTH),
                               lambda bi, i: (bi, i, _MEMQ_BLK * LANES // BRANCH_WIDTH)),
                  pl.BlockSpec((1, m, 2 * BRANCH_WIDTH), lambda bi, i: (bi, 0, 0))],
        out_specs=pl.BlockSpec((1, tq, BRANCH_WIDTH), lambda bi, i: (bi, i, 0)),
        compiler_params=_cparams(("parallel", "arbitrary")),
        name="mem_attention",
    )(proj, mem_kv)


def _layer_norm(r, g, b):
    mu = jnp.mean(r, axis=-1, keepdims=True)
    rc = r - mu
    var = jnp.mean(rc * rc, axis=-1, keepdims=True)
    return rc * lax.rsqrt(var + LN_EPS) * g + b


def _merge_kernel(x_ref, g0_ref, g1_ref, g2_ref, ysb_ref, ydf_ref, ymem_ref, bg_ref, wb_ref,
                  wo_ref, lng_ref, lnb_ref, rwh_ref, rwl_ref, rb_ref, xn_ref, xg_ref, tw_ref, ti_ref):
    merged = None
    for br, (g_ref, y_ref) in enumerate(((g0_ref, ysb_ref), (g1_ref, ydf_ref), (g2_ref, ymem_ref))):
        gate = jax.nn.sigmoid(g_ref[...].astype(F32) + bg_ref[br:br + 1, :])
        term = gate * jnp.dot(y_ref[...], wb_ref[br], preferred_element_type=F32)
        merged = term if merged is None else merged + term
    h = jnp.dot(merged.astype(BF16), wo_ref[...], preferred_element_type=F32)
    xn = _layer_norm(DEEPNORM_ALPHA * x_ref[...].astype(F32) + h, lng_ref[...], lnb_ref[...])
    xn_ref[...] = xn
    _store_row_tiles(xg_ref, xn)

    xh = xn.astype(BF16)
    xl = (xn - xh.astype(F32)).astype(BF16)
    logits = (jnp.dot(xh, rwh_ref[...], preferred_element_type=F32)
              + jnp.dot(xl, rwh_ref[...], preferred_element_type=F32)
              + jnp.dot(xh, rwl_ref[...], preferred_element_type=F32)) + rb_ref[...]
    tm = logits.shape[0]
    eid = lax.broadcasted_iota(jnp.int32, (tm, N_EXPERTS), 1).astype(F32)
    cur = logits
    vals, idxs = [], []
    for _ in range(TOP_K):
        best = jnp.max(cur, axis=-1, keepdims=True)
        idx = jnp.min(jnp.where(cur == best, eid, float(N_EXPERTS)), axis=-1, keepdims=True)
        vals.append(best)
        idxs.append(idx)
        cur = jnp.where(eid == idx, -jnp.inf, cur)
    exps = [jnp.exp(v - vals[0]) for v in vals]
    den = exps[0] + exps[1] + exps[2] + exps[3]
    lane = lax.broadcasted_iota(jnp.int32, (tm, LANES), 1)
    tw = jnp.zeros((tm, LANES), F32)
    ti = jnp.zeros((tm, LANES), F32)
    for k in range(TOP_K):
        tw = jnp.where(lane == k, exps[k] / den, tw)
        ti = jnp.where(lane == k, idxs[k], ti)
    tw_ref[...] = tw
    ti_ref[...] = ti.astype(jnp.int32)


def _merge(x, proj, ysb, ydf, ymem, b_gate, wb, wo, ln_g, ln_b, rwh, rwl, rb, tm):
    t = x.shape[0]
    tok = lambda blk: pl.BlockSpec((tm, blk[0]), lambda i, c=blk[1]: (i, c))
    full2 = lambda shape: pl.BlockSpec(shape, lambda i: (0, 0))
    return pl.pallas_call(
        _merge_kernel,
        out_shape=(jax.ShapeDtypeStruct((t, D_MODEL), F32),
                   jax.ShapeDtypeStruct((t * ROW_CHUNKS, LANES), F32),
                   jax.ShapeDtypeStruct((t, LANES), F32),
                   jax.ShapeDtypeStruct((t, LANES), jnp.int32)),
        grid=(t // tm,),
        in_specs=[tok((D_MODEL, 0)),
                  tok((D_MODEL, 0)), tok((D_MODEL, 1)), tok((D_MODEL, 2)),
                  tok((BRANCH_WIDTH, 0)), tok((BRANCH_WIDTH, 0)), tok((BRANCH_WIDTH, 0)),
                  full2((N_BRANCHES, D_MODEL)),
                  pl.BlockSpec((N_BRANCHES, BRANCH_WIDTH, D_MODEL), lambda i: (0, 0, 0)),
                  full2((D_MODEL, D_MODEL)),
                  full2((1, D_MODEL)), full2((1, D_MODEL)),
                  full2((D_MODEL, N_EXPERTS)), full2((D_MODEL, N_EXPERTS)),
                  full2((1, N_EXPERTS))],
        out_specs=(pl.BlockSpec((tm, D_MODEL), lambda i: (i, 0)),
                   pl.BlockSpec((tm * ROW_CHUNKS, LANES), lambda i: (i, 0)),
                   pl.BlockSpec((tm, LANES), lambda i: (i, 0)),
                   pl.BlockSpec((tm, LANES), lambda i: (i, 0))),
        compiler_params=_cparams(("parallel",)),
        name="merge_ln_router",
    )(x, proj, proj, proj, ysb, ydf, ymem, b_gate, wb, wo, ln_g, ln_b, rwh, rwl, rb)


_MOE_LAG = 3


def _moe_kernel(plan_ref, tok_ref, dst_ref, x_hbm, wgu_ref, bgu_ref, wd_ref, bd_ref,
                y_hbm, xbuf0, xbuf1, ybuf0, ybuf1, wgu_bf, wd_bf, sem_in, sem_out,
                *, tm, n_rows):
    s = pl.program_id(0)
    do_gather = plan_ref[1, s] > 0
    do_compute = plan_ref[2, s] > 0
    do_scatter = plan_ref[3, s] > 0
    do_drain = plan_ref[4, s] > 0
    steady = do_gather & do_compute & do_scatter & do_drain
    parity = s & 1

    @pl.when(plan_ref[5, s] > 0)
    def _():
        wgu_bf[...] = wgu_ref[0].astype(BF16)
        wd_bf[...] = wd_ref[0].astype(BF16)

    @pl.when(s == 0)
    def _():
        ybuf1[...] = jnp.zeros_like(ybuf1)
        init = pltpu.make_async_copy(ybuf1, y_hbm.at[pl.ds(n_rows * ROW_CHUNKS, tm * ROW_CHUNKS), :],
                                     sem_out.at[1])
        init.start()
        init.wait()

    def stages(par):
        x_new, y_old = (xbuf0, ybuf0) if par == 0 else (xbuf1, ybuf1)
        x_cur, y_cur = (xbuf1, ybuf1) if par == 0 else (xbuf0, ybuf0)
        new, cur = par, 1 - par

        def gather_row(r):
            t = pl.multiple_of(tok_ref[0, 0, r], ROW_CHUNKS)
            pltpu.make_async_copy(x_hbm.at[pl.ds(t, ROW_CHUNKS), :],
                                  x_new.at[pl.ds(r * ROW_CHUNKS, ROW_CHUNKS), :],
                                  sem_in.at[new]).start(priority=0)

        def scatter_row(r):
            d = pl.multiple_of(dst_ref[0, 0, r], ROW_CHUNKS)
            pltpu.make_async_copy(y_old.at[pl.ds(r * ROW_CHUNKS, ROW_CHUNKS), :],
                                  y_hbm.at[pl.ds(d, ROW_CHUNKS), :], sem_out.at[new]).start(priority=1)

        def wait_gather():
            pltpu.make_async_copy(x_hbm.at[pl.ds(0, tm * ROW_CHUNKS), :], x_cur, sem_in.at[cur]).wait()

        def wait_scatter():
            pltpu.make_async_copy(y_cur, y_hbm.at[pl.ds(0, tm * ROW_CHUNKS), :], sem_out.at[cur]).wait()

        def compute():
            xb = _load_row_tiles(x_cur, tm).astype(BF16)
            h = jnp.dot(xb, wgu_bf[...], preferred_element_type=F32) + bgu_ref[0]
            gate = jnp.minimum(h[:, :D_FF], SWIGLU_LIMIT)
            up = jnp.clip(h[:, D_FF:], -SWIGLU_LIMIT, SWIGLU_LIMIT)
            act = (up + 1.0) * (gate * jax.nn.sigmoid(SWIGLU_ALPHA * gate))
            _store_row_tiles(
                y_cur, jnp.dot(act.astype(BF16), wd_bf[...], preferred_element_type=F32) + bd_ref[0])

        return gather_row, scatter_row, wait_gather, wait_scatter, compute

    for par in range(2):
        gather_row, scatter_row, wait_gather, wait_scatter, compute = stages(par)
        fast = steady & (parity == par)

        @pl.when(fast)
        def _():
            for r in range(tm):
                gather_row(r)
                scatter_row(r)

        @pl.when(fast & (plan_ref[1, s] > -1))
        def _():
            wait_gather()
            wait_scatter()
            compute()

        @pl.when(jnp.logical_not(steady) & (parity == par))
        def _():
            pl.when(do_compute)(wait_gather)
            pl.when(do_drain)(wait_scatter)

            @pl.when(do_gather)
            def _():
                lax.fori_loop(0, tm, lambda r, c: (gather_row(r), c)[1], 0)

            @pl.when(do_scatter)
            def _():
                lax.fori_loop(0, tm, lambda r, c: (scatter_row(r), c)[1], 0)

            pl.when(do_compute)(compute)


def _moe(xg, plan, tok, dst, wgu, bgu, wd, bd, tm):
    t = xg.shape[0] // ROW_CHUNKS
    n_steps = tok.shape[0]
    n_rows = TOP_K * t
    kern = functools.partial(_moe_kernel, tm=tm, n_rows=n_rows)
    grid_spec = pltpu.PrefetchScalarGridSpec(
        num_scalar_prefetch=1,
        grid=(n_steps,),
        in_specs=[pl.BlockSpec((1, 1, tm), lambda s, plan: (s, 0, 0), memory_space=pltpu.SMEM),
                  pl.BlockSpec((1, 1, tm), lambda s, plan: (s, 0, 0), memory_space=pltpu.SMEM),
                  pl.BlockSpec(memory_space=pl.ANY),
                  pl.BlockSpec((1, D_MODEL, 2 * D_FF), lambda s, plan: (plan[0, s], 0, 0)),
                  pl.BlockSpec((1, 1, 2 * D_FF), lambda s, plan: (plan[0, s], 0, 0)),
                  pl.BlockSpec((1, D_FF, D_MODEL), lambda s, plan: (plan[0, s], 0, 0)),
                  pl.BlockSpec((1, 1, D_MODEL), lambda s, plan: (plan[0, s], 0, 0))],
        out_specs=pl.BlockSpec(memory_space=pl.ANY),
        scratch_shapes=[pltpu.VMEM((tm * ROW_CHUNKS, LANES), F32),
                        pltpu.VMEM((tm * ROW_CHUNKS, LANES), F32),
                        pltpu.VMEM((tm * ROW_CHUNKS, LANES), F32),
                        pltpu.VMEM((tm * ROW_CHUNKS, LANES), F32),
                        pltpu.VMEM((D_MODEL, 2 * D_FF), BF16),
                        pltpu.VMEM((D_FF, D_MODEL), BF16),
                        pltpu.SemaphoreType.DMA((2,)),
                        pltpu.SemaphoreType.DMA((2,))],
    )
    return pl.pallas_call(
        kern,
        out_shape=jax.ShapeDtypeStruct(((n_rows + tm) * ROW_CHUNKS, LANES), F32),
        grid_spec=grid_spec,
        compiler_params=_cparams(("arbitrary",)),
        name="moe_experts",
    )(plan, tok, dst, xg, wgu, bgu, wd, bd)


def _route(topi, t, tm):
    n_rows = TOP_K * t
    e_flat = topi[:, :TOP_K].T.reshape(-1)
    order = jnp.argsort(e_flat, stable=True).astype(jnp.int32)
    experts = jnp.arange(N_EXPERTS, dtype=jnp.int32)
    counts = jnp.sum((e_flat[:, None] == experts[None, :]).astype(jnp.int32), axis=0)
    off = jnp.cumsum(counts) - counts
    tiles = (counts + tm - 1) // tm
    tile_end = jnp.cumsum(tiles)
    tile_off = tile_end - tiles
    total = tile_end[-1]
    n_tiles = n_rows // tm + N_EXPERTS
    g = jnp.arange(n_tiles, dtype=jnp.int32)
    te = jnp.minimum(jnp.sum((g[:, None] >= tile_end[None, :]).astype(jnp.int32), axis=1),
                     N_EXPERTS - 1)
    r = jnp.arange(tm, dtype=jnp.int32)
    start = (g - tile_off[te]) * tm
    nv = jnp.where(g < total, jnp.clip(counts[te] - start, 0, tm), 0).astype(jnp.int32)
    local = start[:, None] + r[None, :]
    valid = r[None, :] < nv[:, None]
    first = jnp.minimum(off[te] + start, n_rows)
    order_tail = jnp.concatenate([order, jnp.zeros((tm,), jnp.int32)])
    f = order_tail[first[:, None] + r[None, :]]
    tok = (jnp.where(valid, f % t, 0) * ROW_CHUNKS).astype(jnp.int32)
    dst = (jnp.where(valid, f, n_rows + r[None, :]) * ROW_CHUNKS).astype(jnp.int32)

    def lagged(a, lag):
        return jnp.pad(a, ((lag, _MOE_LAG - lag),) + ((0, 0),) * (a.ndim - 1))

    te_step = jnp.pad(te, (1, _MOE_LAG - 1), mode="edge")
    fresh = jnp.concatenate([jnp.ones((2,), jnp.int32),
                             (te_step[2:] != te_step[1:-1]).astype(jnp.int32)])
    plan = jnp.stack([te_step, lagged(nv, 0), lagged(nv, 1), lagged(nv, 2), lagged(nv, 3), fresh])
    n_steps = n_tiles + _MOE_LAG
    return (plan, lagged(tok, 0).reshape(n_steps, 1, tm), lagged(dst, 2).reshape(n_steps, 1, tm))


def _combine_kernel(xn_ref, tw_ref, y0_ref, y1_ref, y2_ref, y3_ref, lng_ref, lnb_ref, o_ref):
    tw = tw_ref[...]
    f = None
    for k, y_ref in enumerate((y0_ref, y1_ref, y2_ref, y3_ref)):
        term = tw[:, k:k + 1] * _load_row_tiles(y_ref, tw.shape[0])
        f = term if f is None else f + term
    o_ref[...] = _layer_norm(DEEPNORM_ALPHA * xn_ref[...] + f, lng_ref[...], lnb_ref[...])


def _combine(xn, tw, yb, ln_g, ln_b, tm):
    t = xn.shape[0]
    nblk = t // tm
    ysp = lambda k: pl.BlockSpec((tm * ROW_CHUNKS, LANES), lambda i, k=k: (k * nblk + i, 0))
    return pl.pallas_call(
        _combine_kernel,
        out_shape=jax.ShapeDtypeStruct((t, D_MODEL), F32),
        grid=(nblk,),
        in_specs=[pl.BlockSpec((tm, D_MODEL), lambda i: (i, 0)),
                  pl.BlockSpec((tm, LANES), lambda i: (i, 0)),
                  ysp(0), ysp(1), ysp(2), ysp(3),
                  pl.BlockSpec((1, D_MODEL), lambda i: (0, 0)),
                  pl.BlockSpec((1, D_MODEL), lambda i: (0, 0))],
        out_specs=pl.BlockSpec((tm, D_MODEL), lambda i: (i, 0)),
        compiler_params=_cparams(("parallel",)),
        name="combine_ln",
    )(xn, tw, yb, yb, yb, yb, ln_g, ln_b)


def kernel(x, mem, w_in, b_gate, diff_lambda, diff_subln_g, rel_bias, w_mem_kv, w_branch, w_out,
           ln1_g, ln1_b, router_w, router_b, w_gate_up, b_gate_up, w_down, b_down, ln2_g, ln2_b):
    b, s, d = x.shape
    t = b * s
    m = mem.shape[1]
    tq = min(ATT_TILE, s)
    tok_tile = min(TOK_TILE, t)

    idx = jnp.arange(tq, dtype=jnp.int32)
    ut = (idx[None, :] >= idx[:, None]).astype(BF16)
    bias = _bias_tiles(rel_bias, tq)
    mem2 = mem.reshape(b * m, d)
    xt = x.reshape(t, d)

    for l in range(DEPTH):
        w_in_l = jnp.concatenate([w_in[l][:, 7 * BRANCH_WIDTH:], w_in[l][:, :7 * BRANCH_WIDTH]],
                                 axis=1).astype(BF16)
        proj = _matmul(xt, w_in_l, min(1024, t), IN_WIDTH // 2, "in_proj")
        proj3 = proj.reshape(b, s, IN_WIDTH)
        mem_kv = _matmul(mem2, w_mem_kv[l].astype(BF16), min(1024, b * m), 512, "mem_kv")
        mem_kv = mem_kv.reshape(b, m, 2 * BRANCH_WIDTH)

        lam_init = 0.8 - 0.6 * math.exp(-0.3 * l)
        sbv_t = _key_major(proj3[:, :, _SBV_BLK * LANES:_SBV_BLK * LANES + BRANCH_WIDTH], tq)
        dfv_t = _key_major(proj3[:, :, _DFV_BLK * LANES:_DFV_BLK * LANES + BRANCH_WIDTH], tq)
        ysb = _sb_attention(proj3, sbv_t, ut, tq)
        ydf = _df_attention(proj3, dfv_t, diff_lambda[l].astype(F32),
                            diff_subln_g[l].reshape(1, 2 * DIFF_HEAD_DIM).astype(F32),
                            bias, tq, lam_init)
        ymem = _mem_attention(proj3, mem_kv, min(512, s))

        rw = router_w[l].astype(F32)
        rwh = rw.astype(BF16)
        rwl = (rw - rwh.astype(F32)).astype(BF16)
        xn, xg, tw, ti = _merge(
            xt, proj, ysb.reshape(t, BRANCH_WIDTH), ydf.reshape(t, BRANCH_WIDTH),
            ymem.reshape(t, BRANCH_WIDTH), b_gate[l].reshape(N_BRANCHES, D_MODEL).astype(F32),
            w_branch[l].astype(BF16), w_out[l].astype(BF16),
            ln1_g[l].reshape(1, d).astype(F32), ln1_b[l].reshape(1, d).astype(F32),
            rwh, rwl, router_b[l].reshape(1, N_EXPERTS).astype(F32), tok_tile)

        plan, tok, dst = _route(ti, t, MOE_TILE)
        yb = _moe(xg, plan, tok, dst, w_gate_up[l],
                  b_gate_up[l].reshape(N_EXPERTS, 1, 2 * D_FF).astype(F32),
                  w_down[l], b_down[l].reshape(N_EXPERTS, 1, D_MODEL).astype(F32),
                  MOE_TILE)
        xt = _combine(xn, tw, yb, ln2_g[l].reshape(1, d).astype(F32),
                      ln2_b[l].reshape(1, d).astype(F32), tok_tile)
    return xt.reshape(b, s, d)
```

```python
import functools
import math

import jax
import jax.numpy as jnp
from jax import lax
from jax.experimental import pallas as pl
from jax.experimental.pallas import tpu as pltpu

F32 = jnp.float32
BF16 = jnp.bfloat16

D_MODEL = 1024
DEPTH = 2
CHUNK = 64
BRANCH_WIDTH = D_MODEL // 2
SB_HEAD_DIM = 64
DIFF_HEAD_DIM = 64
DIFF_HEADS = BRANCH_WIDTH // (2 * DIFF_HEAD_DIM)
MEM_HEAD_DIM = 128
MEM_HEADS = BRANCH_WIDTH // MEM_HEAD_DIM
N_BRANCHES = 3
GATE_WIDTH = N_BRANCHES * D_MODEL
IN_WIDTH = 7 * BRANCH_WIDTH + GATE_WIDTH
NUM_BUCKETS = 32
MAX_DISTANCE = 128
N_EXPERTS = 32
TOP_K = 4
D_FF = D_MODEL
SWIGLU_LIMIT = 7.0
SWIGLU_ALPHA = 1.702
LN_EPS = 1e-5
RMS_EPS = 1e-5
DEEPNORM_ALPHA = (2 * DEPTH) ** 0.25
LOG2E = math.log2(math.e)

LANES = 128
SUBLANES = 8
ROW_CHUNKS = D_MODEL // LANES
assert ROW_CHUNKS == SUBLANES
NEG_BIG = -1e30

_GATE_BLK = 0
_SBQ_BLK = GATE_WIDTH // LANES
_SBK_BLK = _SBQ_BLK + 4
_SBV_BLK = _SBQ_BLK + 8
_DFQ_BLK = _SBQ_BLK + 12
_DFK_BLK = _SBQ_BLK + 16
_DFV_BLK = _SBQ_BLK + 20
_MEMQ_BLK = _SBQ_BLK + 24

VMEM_LIMIT = 56 * 1024 * 1024

ATT_TILE = 256
MOE_TILE = 256
TOK_TILE = 512


def _cparams(sem):
    return pltpu.CompilerParams(dimension_semantics=sem, vmem_limit_bytes=VMEM_LIMIT)


def _load_row_tiles(ref, n):
    return jnp.concatenate([ref[pl.ds(c, n, stride=ROW_CHUNKS), :] for c in range(ROW_CHUNKS)],
                           axis=1)


def _store_row_tiles(ref, value):
    n = value.shape[0]
    for c in range(ROW_CHUNKS):
        ref[pl.ds(c, n, stride=ROW_CHUNKS), :] = value[:, c * LANES:(c + 1) * LANES]


def _matmul_kernel(a_ref, w_ref, o_ref):
    a = a_ref[...].astype(BF16)
    o_ref[...] = jnp.dot(a, w_ref[...], preferred_element_type=F32).astype(o_ref.dtype)


def _matmul(a, w, tm, tn, name):
    m, k = a.shape
    n = w.shape[1]
    return pl.pallas_call(
        _matmul_kernel,
        out_shape=jax.ShapeDtypeStruct((m, n), BF16),
        grid=(m // tm, n // tn),
        in_specs=[pl.BlockSpec((tm, k), lambda i, j: (i, 0)),
                  pl.BlockSpec((k, tn), lambda i, j: (0, j))],
        out_specs=pl.BlockSpec((tm, tn), lambda i, j: (i, j)),
        compiler_params=_cparams(("parallel", "arbitrary")),
        name=name,
    )(a, w)


_IN_PROJ_CHUNKS = 4


def _in_proj_kernel(a_ref, w_ref, o_ref, sbvt_ref, dfvt_ref, *, tq):
    tm = a_ref.shape[0]
    a = a_ref[...].astype(BF16)
    cw = IN_WIDTH // _IN_PROJ_CHUNKS
    for c in range(_IN_PROJ_CHUNKS):
        o_ref[:, c * cw:(c + 1) * cw] = jnp.dot(
            a, w_ref[:, c * cw:(c + 1) * cw], preferred_element_type=F32).astype(o_ref.dtype)
    for vt_ref, blk in ((sbvt_ref, _SBV_BLK), (dfvt_ref, _DFV_BLK)):
        for sub in range(tm // tq):
            v = o_ref[sub * tq:(sub + 1) * tq, blk * LANES:blk * LANES + BRANCH_WIDTH]
            vt_ref[0, sub] = v.astype(F32).T.astype(vt_ref.dtype)


def _in_proj(x, w, b, s, tm, tq):
    t, k = x.shape
    per_batch = s // tm
    vt_shape = jax.ShapeDtypeStruct((b, s // tq, BRANCH_WIDTH, tq), BF16)
    vt_spec = pl.BlockSpec((1, tm // tq, BRANCH_WIDTH, tq),
                           lambda i: (i // per_batch, i % per_batch, 0, 0))
    return pl.pallas_call(
        functools.partial(_in_proj_kernel, tq=tq),
        out_shape=(jax.ShapeDtypeStruct((t, IN_WIDTH), BF16), vt_shape, vt_shape),
        grid=(t // tm,),
        in_specs=[pl.BlockSpec((tm, k), lambda i: (i, 0)),
                  pl.BlockSpec((k, IN_WIDTH), lambda i: (0, 0), pipeline_mode=pl.Buffered(1))],
        out_specs=(pl.BlockSpec((tm, IN_WIDTH), lambda i: (i, 0)), vt_spec, vt_spec),
        compiler_params=_cparams(("parallel",)),
        name="in_proj",
    )(x, w)


_NT = (((1,), (1,)), ((), ()))


def _sb_kernel(q_ref, k_ref, vt_ref, ut_ref, o_ref, c_ref, acc_ref, zn_ref, a_ref, *, tq):
    i = pl.program_id(1)
    n_heads = 2 * (BRANCH_WIDTH // LANES)
    low = lax.broadcasted_iota(jnp.int32, (tq, LANES), 1) < SB_HEAD_DIM
    q_heads = []
    for hp in range(n_heads // 2):
        qs = q_ref[0, :, hp * LANES:(hp + 1) * LANES].astype(F32) * (-LOG2E * SB_HEAD_DIM ** -0.5)
        q_heads += [jnp.where(low, qs, 0.0).astype(BF16), jnp.where(low, 0.0, qs).astype(BF16)]
    ut = ut_ref[...]
    key = lax.broadcasted_iota(jnp.int32, (tq, tq), 0)
    qry = lax.broadcasted_iota(jnp.int32, (tq, tq), 1)
    earlier = key < qry

    c_ref[...] = jnp.zeros_like(c_ref)
    acc_ref[...] = jnp.zeros_like(acc_ref)

    def scores(j):
        start = pl.multiple_of(j * tq, tq)
        for h in range(n_heads):
            kb = k_ref[0, pl.ds(start, tq), (h // 2) * LANES:(h // 2 + 1) * LANES]
            zn_ref[h] = lax.dot_general(kb, q_heads[h], _NT, preferred_element_type=F32)

    def weights(masked):
        l1ms = []
        for h in range(n_heads):
            zn = zn_ref[h]
            neg_abs = lax.bitcast_convert_type(
                lax.bitcast_convert_type(zn, jnp.uint32) | jnp.uint32(0x80000000), F32)
            l1m = jnp.minimum(zn, 0.0) - jnp.log2(1.0 + jnp.exp2(neg_abs))
            if masked:
                l1m = jnp.where(earlier, l1m, 0.0)
            l1ms.append(l1m.astype(BF16))
        incls = [jnp.dot(ut, l1ms[h], preferred_element_type=F32) for h in range(n_heads)]
        for h in range(n_heads):
            c = c_ref[h:h + 1, :]
            a = jnp.exp2(incls[h] + c - zn_ref[h])
            if masked:
                a = jnp.where(earlier, a, 0.0)
            a_ref[h] = a.astype(BF16)
            c_ref[h:h + 1, :] = c + incls[h][0:1, :]

    def values(j):
        for h in range(n_heads):
            vtb = vt_ref[0, j, (h // 2) * LANES:(h // 2 + 1) * LANES, :]
            acc_ref[h] += jnp.dot(vtb, a_ref[h], preferred_element_type=F32)

    scores(i)
    weights(True)
    scores(jnp.maximum(i - 1, 0))

    def body(n, carry):
        j = i - 1 - n
        values(j + 1)
        weights(False)
        scores(jnp.maximum(j - 1, 0))
        return carry

    lax.fori_loop(0, i, body, 0)
    values(0)
    sub = lax.broadcasted_iota(jnp.int32, (LANES, tq), 0)
    for hp in range(n_heads // 2):
        ot = jnp.where(sub < SB_HEAD_DIM, acc_ref[2 * hp], acc_ref[2 * hp + 1])
        o_ref[0, :, hp * LANES:(hp + 1) * LANES] = ot.T.astype(o_ref.dtype)


def _sb_attention(proj, vt, ut, tq):
    b, s, _ = proj.shape
    n_heads = 2 * (BRANCH_WIDTH // LANES)
    wblk = BRANCH_WIDTH // LANES
    kern = functools.partial(_sb_kernel, tq=tq)
    return pl.pallas_call(
        kern,
        out_shape=jax.ShapeDtypeStruct((b, s, BRANCH_WIDTH), BF16),
        grid=(b, s // tq),
        in_specs=[pl.BlockSpec((1, tq, BRANCH_WIDTH), lambda bi, i: (bi, i, _SBQ_BLK // wblk)),
                  pl.BlockSpec((1, s, BRANCH_WIDTH), lambda bi, i: (bi, 0, _SBK_BLK // wblk)),
                  pl.BlockSpec((1, s // tq, BRANCH_WIDTH, tq), lambda bi, i: (bi, 0, 0, 0)),
                  pl.BlockSpec((tq, tq), lambda bi, i: (0, 0))],
        out_specs=pl.BlockSpec((1, tq, BRANCH_WIDTH), lambda bi, i: (bi, i, 0)),
        scratch_shapes=[pltpu.VMEM((n_heads, tq), F32),
                        pltpu.VMEM((n_heads, LANES, tq), F32),
                        pltpu.VMEM((n_heads, tq, tq), F32),
                        pltpu.VMEM((n_heads, tq, tq), BF16)],
        compiler_params=_cparams(("parallel", "arbitrary")),
        name="sb_attention",
    )(proj, proj, vt, ut)


def _df_kernel(lam_ref, g_ref, q_ref, k_ref, vt_ref, bias_ref, o_ref, mx_ref, den_ref, acc_ref,
               sc_ref, p_ref, *, tq, lam_init):
    i = pl.program_id(1)
    n_maps = 2 * DIFF_HEADS
    low = lax.broadcasted_iota(jnp.int32, (tq, LANES), 1) < DIFF_HEAD_DIM
    q_maps = []
    for h in range(DIFF_HEADS):
        qs = q_ref[0, :, h * LANES:(h + 1) * LANES].astype(F32) * (LOG2E * DIFF_HEAD_DIM ** -0.5)
        q_maps += [jnp.where(low, qs, 0.0).astype(BF16), jnp.where(low, 0.0, qs).astype(BF16)]

    mx_ref[...] = jnp.full_like(mx_ref, NEG_BIG)
    den_ref[...] = jnp.zeros_like(den_ref)
    acc_ref[...] = jnp.zeros_like(acc_ref)

    def scores(j):
        start = pl.multiple_of(j * tq, tq)
        for m in range(n_maps):
            kb = k_ref[0, pl.ds(start, tq), (m // 2) * LANES:(m // 2 + 1) * LANES]
            sc_ref[m] = lax.dot_general(kb, q_maps[m], _NT, preferred_element_type=F32)

    def values(j):
        return [jnp.dot(vt_ref[0, j, (m // 2) * LANES:(m // 2 + 1) * LANES, :], p_ref[m],
                        preferred_element_type=F32) for m in range(n_maps)]

    def softmax(pvs, kind, off=None):
        for m in range(n_maps):
            sc = sc_ref[m]
            if kind is not None:
                sc = sc + bias_ref[m, kind]
            if off is not None:
                sc = sc + off
            mx = mx_ref[m:m + 1, :]
            mx_new = jnp.maximum(mx, jnp.max(sc, axis=0, keepdims=True))
            alpha = jnp.exp2(mx - mx_new)
            p = jnp.exp2(sc - mx_new)
            den_ref[m:m + 1, :] = alpha * den_ref[m:m + 1, :] + jnp.sum(p, axis=0, keepdims=True)
            mx_ref[m:m + 1, :] = mx_new
            p_ref[m] = p.astype(BF16)
            if pvs is not None:
                acc_ref[m] = alpha * (acc_ref[m] + pvs[m])

    scores(i)
    softmax(None, 0)
    pvs = values(i)
    scores(jnp.maximum(i - 1, 0))
    softmax(pvs, 1, jnp.where(i == 0, NEG_BIG, 0.0).astype(F32))
    scores(jnp.maximum(i - 2, 0))

    def body(n, carry):
        j = i - 2 - n
        pvs = values(j + 1)
        softmax(pvs, None)
        scores(jnp.maximum(j - 1, 0))
        return carry

    lax.fori_loop(0, jnp.maximum(i - 1, 0), body, 0)
    pvs = values(0)
    for m in range(n_maps):
        acc_ref[m] += pvs[m]


    lf = lam_ref[...]
    lam = (jnp.exp(jnp.sum(lf[0:1] * lf[1:2], axis=-1, keepdims=True))
           - jnp.exp(jnp.sum(lf[2:3] * lf[3:4], axis=-1, keepdims=True)) + lam_init)
    for h in range(DIFF_HEADS):
        m0, m1 = 2 * h, 2 * h + 1
        o = (acc_ref[m0] / den_ref[m0:m0 + 1, :]
             - lam * (acc_ref[m1] / den_ref[m1:m1 + 1, :]))
        o = o * lax.rsqrt(jnp.mean(o * o, axis=0, keepdims=True) + RMS_EPS)
        o_ref[0, :, h * LANES:(h + 1) * LANES] = (
            o.T * g_ref[...] * (1.0 - lam_init)).astype(o_ref.dtype)


def _df_attention(proj, vt, lam_rows, subln_g, bias, tq, lam_init):
    b, s, _ = proj.shape
    n_maps = 2 * DIFF_HEADS
    wblk = BRANCH_WIDTH // LANES
    kern = functools.partial(_df_kernel, tq=tq, lam_init=lam_init)
    return pl.pallas_call(
        kern,
        out_shape=jax.ShapeDtypeStruct((b, s, BRANCH_WIDTH), BF16),
        grid=(b, s // tq),
        in_specs=[pl.BlockSpec((4, DIFF_HEAD_DIM), lambda bi, i: (0, 0)),
                  pl.BlockSpec((1, LANES), lambda bi, i: (0, 0)),
                  pl.BlockSpec((1, tq, BRANCH_WIDTH), lambda bi, i: (bi, i, _DFQ_BLK // wblk)),
                  pl.BlockSpec((1, s, BRANCH_WIDTH), lambda bi, i: (bi, 0, _DFK_BLK // wblk)),
                  pl.BlockSpec((1, s // tq, BRANCH_WIDTH, tq), lambda bi, i: (bi, 0, 0, 0)),
                  pl.BlockSpec((n_maps, 2, tq, tq), lambda bi, i: (0, 0, 0, 0))],
        out_specs=pl.BlockSpec((1, tq, BRANCH_WIDTH), lambda bi, i: (bi, i, 0)),
        scratch_shapes=[pltpu.VMEM((n_maps, tq), F32),
                        pltpu.VMEM((n_maps, tq), F32),
                        pltpu.VMEM((n_maps, LANES, tq), F32),
                        pltpu.VMEM((n_maps, tq, tq), F32),
                        pltpu.VMEM((n_maps, tq, tq), BF16)],
        compiler_params=_cparams(("parallel", "arbitrary")),
        name="diff_attention",
    )(lam_rows, subln_g, proj, proj, vt, bias)


def _t5_bucket(rel):
    half = NUM_BUCKETS // 2
    max_exact = half // 2
    n = jnp.abs(rel)
    nf = jnp.maximum(n, 1).astype(F32)
    large = max_exact + (jnp.log(nf / max_exact) / math.log(MAX_DISTANCE / max_exact)
                         * (half - max_exact)).astype(jnp.int32)
    large = jnp.minimum(large, half - 1)
    return jnp.where(rel > 0, half, 0) + jnp.where(n < max_exact, n, large)


def _bias_tiles(rel_bias, tq):
    table = rel_bias.astype(F32)
    r = jnp.arange(tq, dtype=jnp.int32)
    rel_diag = r[None, :] - r[:, None]
    far = table[_t5_bucket(jnp.full((), -(tq + 1), jnp.int32))]
    diag = (table[_t5_bucket(rel_diag)] - far) * LOG2E
    visible = (r[None, :] // CHUNK) <= (r[:, None] // CHUNK)
    diag = jnp.where(visible[..., None], diag, NEG_BIG)
    near = (table[_t5_bucket(rel_diag - tq)] - far) * LOG2E
    return jnp.stack([diag, near], axis=0).transpose(3, 0, 2, 1)


def _mem_kernel(q_ref, kv_ref, o_ref):
    for h in range(MEM_HEADS):
        lo = h * MEM_HEAD_DIM
        q = (q_ref[0, :, lo:lo + MEM_HEAD_DIM].astype(F32) * (MEM_HEAD_DIM ** -0.5)).astype(BF16)
        k = kv_ref[0, :, lo:lo + MEM_HEAD_DIM]
        v = kv_ref[0, :, BRANCH_WIDTH + lo:BRANCH_WIDTH + lo + MEM_HEAD_DIM]
        sc = lax.dot_general(q, k, (((1,), (1,)), ((), ())), preferred_element_type=F32)
        p = jnp.exp(sc - jnp.max(sc, axis=-1, keepdims=True))
        den = jnp.sum(p, axis=-1, keepdims=True)
        o = jnp.dot(p.astype(BF16), v, preferred_element_type=F32) / den
        o_ref[0, :, lo:lo + MEM_HEAD_DIM] = o.astype(o_ref.dtype)


def _mem_attention(proj, mem_kv, tq):
    b, s, _ = proj.shape
    m = mem_kv.shape[1]
    return pl.pallas_call(
        _mem_kernel,
        out_shape=jax.ShapeDtypeStruct((b, s, BRANCH_WIDTH), BF16),
        grid=(b, s // tq),
        in_specs=[pl.BlockSpec((1, tq, BRANCH_WIDTH),
                               lambda bi, i: (bi, i, _MEMQ_BLK * LANES // BRANCH_WIDTH)),
                  pl.BlockSpec((1, m, 2 * BRANCH_WIDTH), lambda bi, i: (bi, 0, 0))],
        out_specs=pl.BlockSpec((1, tq, BRANCH_WIDTH), lambda bi, i: (bi, i, 0)),
        compiler_params=_cparams(("parallel", "arbitrary")),
        name="mem_attention",
    )(proj, mem_kv)


def _layer_norm(r, g, b):
    mu = jnp.mean(r, axis=-1, keepdims=True)
    rc = r - mu
    var = jnp.mean(rc * rc, axis=-1, keepdims=True)
    return rc * lax.rsqrt(var + LN_EPS) * g + b


def _merge_kernel(x_ref, g0_ref, g1_ref, g2_ref, ysb_ref, ydf_ref, ymem_ref, bg_ref, wb_ref,
                  wo_ref, lng_ref, lnb_ref, rwh_ref, rwl_ref, rb_ref, xn_ref, xg_ref, tw_ref, ti_ref):
    merged = None
    for br, (g_ref, y_ref) in enumerate(((g0_ref, ysb_ref), (g1_ref, ydf_ref), (g2_ref, ymem_ref))):
        gate = jax.nn.sigmoid(g_ref[...].astype(F32) + bg_ref[br:br + 1, :])
        term = gate * jnp.dot(y_ref[...], wb_ref[br], preferred_element_type=F32)
        merged = term if merged is None else merged + term
    h = jnp.dot(merged.astype(BF16), wo_ref[...], preferred_element_type=F32)
    xn = _layer_norm(DEEPNORM_ALPHA * x_ref[...].astype(F32) + h, lng_ref[...], lnb_ref[...])
    xn_ref[...] = xn
    _store_row_tiles(xg_ref, xn)

    xh = xn.astype(BF16)
    xl = (xn - xh.astype(F32)).astype(BF16)
    logits = (jnp.dot(xh, rwh_ref[...], preferred_element_type=F32)
              + jnp.dot(xl, rwh_ref[...], preferred_element_type=F32)
              + jnp.dot(xh, rwl_ref[...], preferred_element_type=F32)) + rb_ref[...]
    tm = logits.shape[0]
    eid = lax.broadcasted_iota(jnp.int32, (tm, N_EXPERTS), 1).astype(F32)
    cur = logits
    vals, idxs = [], []
    for _ in range(TOP_K):
        best = jnp.max(cur, axis=-1, keepdims=True)
        idx = jnp.min(jnp.where(cur == best, eid, float(N_EXPERTS)), axis=-1, keepdims=True)
        vals.append(best)
        idxs.append(idx)
        cur = jnp.where(eid == idx, -jnp.inf, cur)
    exps = [jnp.exp(v - vals[0]) for v in vals]
    den = exps[0] + exps[1] + exps[2] + exps[3]
    lane = lax.broadcasted_iota(jnp.int32, (tm, LANES), 1)
    tw = jnp.zeros((tm, LANES), F32)
    ti = jnp.zeros((tm, LANES), F32)
    for k in range(TOP_K):
        tw = jnp.where(lane == k, exps[k] / den, tw)
        ti = jnp.where(lane == k, idxs[k], ti)
    tw_ref[...] = tw
    ti_ref[...] = ti.astype(jnp.int32)


def _merge(x, proj, ysb, ydf, ymem, b_gate, wb, wo, ln_g, ln_b, rwh, rwl, rb, tm):
    t = x.shape[0]
    tok = lambda blk: pl.BlockSpec((tm, blk[0]), lambda i, c=blk[1]: (i, c))
    full2 = lambda shape: pl.BlockSpec(shape, lambda i: (0, 0))
    return pl.pallas_call(
        _merge_kernel,
        out_shape=(jax.ShapeDtypeStruct((t, D_MODEL), F32),
                   jax.ShapeDtypeStruct((t * ROW_CHUNKS, LANES), F32),
                   jax.ShapeDtypeStruct((t, LANES), F32),
                   jax.ShapeDtypeStruct((t, LANES), jnp.int32)),
        grid=(t // tm,),
        in_specs=[tok((D_MODEL, 0)),
                  tok((D_MODEL, 0)), tok((D_MODEL, 1)), tok((D_MODEL, 2)),
                  tok((BRANCH_WIDTH, 0)), tok((BRANCH_WIDTH, 0)), tok((BRANCH_WIDTH, 0)),
                  full2((N_BRANCHES, D_MODEL)),
                  pl.BlockSpec((N_BRANCHES, BRANCH_WIDTH, D_MODEL), lambda i: (0, 0, 0)),
                  full2((D_MODEL, D_MODEL)),
                  full2((1, D_MODEL)), full2((1, D_MODEL)),
                  full2((D_MODEL, N_EXPERTS)), full2((D_MODEL, N_EXPERTS)),
                  full2((1, N_EXPERTS))],
        out_specs=(pl.BlockSpec((tm, D_MODEL), lambda i: (i, 0)),
                   pl.BlockSpec((tm * ROW_CHUNKS, LANES), lambda i: (i, 0)),
                   pl.BlockSpec((tm, LANES), lambda i: (i, 0)),
                   pl.BlockSpec((tm, LANES), lambda i: (i, 0))),
        compiler_params=_cparams(("parallel",)),
        name="merge_ln_router",
    )(x, proj, proj, proj, ysb, ydf, ymem, b_gate, wb, wo, ln_g, ln_b, rwh, rwl, rb)


_MOE_LAG = 3


def _moe_kernel(plan_ref, tok_ref, dst_ref, x_hbm, wgu_ref, bgu_ref, wd_ref, bd_ref,
                y_hbm, xbuf0, xbuf1, ybuf0, ybuf1, wgu_bf, wd_bf, sem_in, sem_out,
                *, tm, n_rows):
    s = pl.program_id(0)
    do_gather = plan_ref[1, s] > 0
    do_compute = plan_ref[2, s] > 0
    do_scatter = plan_ref[3, s] > 0
    do_drain = plan_ref[4, s] > 0
    steady = do_gather & do_compute & do_scatter & do_drain
    parity = s & 1

    @pl.when(plan_ref[5, s] > 0)
    def _():
        wgu_bf[...] = wgu_ref[0].astype(BF16)
        wd_bf[...] = wd_ref[0].astype(BF16)

    @pl.when(s == 0)
    def _():
        ybuf1[...] = jnp.zeros_like(ybuf1)
        init = pltpu.make_async_copy(ybuf1, y_hbm.at[pl.ds(n_rows * ROW_CHUNKS, tm * ROW_CHUNKS), :],
                                     sem_out.at[1])
        init.start()
        init.wait()

    def stages(par):
        x_new, y_old = (xbuf0, ybuf0) if par == 0 else (xbuf1, ybuf1)
        x_cur, y_cur = (xbuf1, ybuf1) if par == 0 else (xbuf0, ybuf0)
        new, cur = par, 1 - par

        def gather_row(r):
            t = pl.multiple_of(tok_ref[0, 0, r], ROW_CHUNKS)
            pltpu.make_async_copy(x_hbm.at[pl.ds(t, ROW_CHUNKS), :],
                                  x_new.at[pl.ds(r * ROW_CHUNKS, ROW_CHUNKS), :],
                                  sem_in.at[new]).start(priority=0)

        def scatter_row(r):
            d = pl.multiple_of(dst_ref[0, 0, r], ROW_CHUNKS)
            pltpu.make_async_copy(y_old.at[pl.ds(r * ROW_CHUNKS, ROW_CHUNKS), :],
                                  y_hbm.at[pl.ds(d, ROW_CHUNKS), :], sem_out.at[new]).start(priority=1)

        def wait_gather():
            pltpu.make_async_copy(x_hbm.at[pl.ds(0, tm * ROW_CHUNKS), :], x_cur, sem_in.at[cur]).wait()

        def wait_scatter():
            pltpu.make_async_copy(y_cur, y_hbm.at[pl.ds(0, tm * ROW_CHUNKS), :], sem_out.at[cur]).wait()

        def compute():
            xb = _load_row_tiles(x_cur, tm).astype(BF16)
            h = jnp.dot(xb, wgu_bf[...], preferred_element_type=F32) + bgu_ref[0]
            gate = jnp.minimum(h[:, :D_FF], SWIGLU_LIMIT)
            up = jnp.clip(h[:, D_FF:], -SWIGLU_LIMIT, SWIGLU_LIMIT)
            act = (up + 1.0) * (gate * jax.nn.sigmoid(SWIGLU_ALPHA * gate))
            _store_row_tiles(
                y_cur, jnp.dot(act.astype(BF16), wd_bf[...], preferred_element_type=F32) + bd_ref[0])

        return gather_row, scatter_row, wait_gather, wait_scatter, compute

    for par in range(2):
        gather_row, scatter_row, wait_gather, wait_scatter, compute = stages(par)
        fast = steady & (parity == par)

        @pl.when(fast)
        def _():
            for r in range(tm):
                gather_row(r)
                scatter_row(r)

        @pl.when(fast & (plan_ref[1, s] > -1))
        def _():
            wait_gather()
            wait_scatter()
            compute()

        @pl.when(jnp.logical_not(steady) & (parity == par))
        def _():
            pl.when(do_compute)(wait_gather)
            pl.when(do_drain)(wait_scatter)

            @pl.when(do_gather)
            def _():
                lax.fori_loop(0, tm, lambda r, c: (gather_row(r), c)[1], 0)

            @pl.when(do_scatter)
            def _():
                lax.fori_loop(0, tm, lambda r, c: (scatter_row(r), c)[1], 0)

            pl.when(do_compute)(compute)


def _moe(xg, plan, tok, dst, wgu, bgu, wd, bd, tm):
    t = xg.shape[0] // ROW_CHUNKS
    n_steps = tok.shape[0]
    n_rows = TOP_K * t
    kern = functools.partial(_moe_kernel, tm=tm, n_rows=n_rows)
    grid_spec = pltpu.PrefetchScalarGridSpec(
        num_scalar_prefetch=1,
        grid=(n_steps,),
        in_specs=[pl.BlockSpec((1, 1, tm), lambda s, plan: (s, 0, 0), memory_space=pltpu.SMEM),
                  pl.BlockSpec((1, 1, tm), lambda s, plan: (s, 0, 0), memory_space=pltpu.SMEM),
                  pl.BlockSpec(memory_space=pl.ANY),
                  pl.BlockSpec((1, D_MODEL, 2 * D_FF), lambda s, plan: (plan[0, s], 0, 0)),
                  pl.BlockSpec((1, 1, 2 * D_FF), lambda s, plan: (plan[0, s], 0, 0)),
                  pl.BlockSpec((1, D_FF, D_MODEL), lambda s, plan: (plan[0, s], 0, 0)),
                  pl.BlockSpec((1, 1, D_MODEL), lambda s, plan: (plan[0, s], 0, 0))],
        out_specs=pl.BlockSpec(memory_space=pl.ANY),
        scratch_shapes=[pltpu.VMEM((tm * ROW_CHUNKS, LANES), F32),
                        pltpu.VMEM((tm * ROW_CHUNKS, LANES), F32),
                        pltpu.VMEM((tm * ROW_CHUNKS, LANES), F32),
                        pltpu.VMEM((tm * ROW_CHUNKS, LANES), F32),
                        pltpu.VMEM((D_MODEL, 2 * D_FF), BF16),
                        pltpu.VMEM((D_FF, D_MODEL), BF16),
                        pltpu.SemaphoreType.DMA((2,)),
                        pltpu.SemaphoreType.DMA((2,))],
    )
    return pl.pallas_call(
        kern,
        out_shape=jax.ShapeDtypeStruct(((n_rows + tm) * ROW_CHUNKS, LANES), F32),
        grid_spec=grid_spec,
        compiler_params=_cparams(("arbitrary",)),
        name="moe_experts",
    )(plan, tok, dst, xg, wgu, bgu, wd, bd)


def _route(topi, t, tm):
    n_rows = TOP_K * t
    e_flat = topi[:, :TOP_K].T.reshape(-1)
    order = jnp.argsort(e_flat, stable=True).astype(jnp.int32)
    experts = jnp.arange(N_EXPERTS, dtype=jnp.int32)
    counts = jnp.sum((e_flat[:, None] == experts[None, :]).astype(jnp.int32), axis=0)
    off = jnp.cumsum(counts) - counts
    tiles = (counts + tm - 1) // tm
    tile_end = jnp.cumsum(tiles)
    tile_off = tile_end - tiles
    total = tile_end[-1]
    n_tiles = n_rows // tm + N_EXPERTS
    g = jnp.arange(n_tiles, dtype=jnp.int32)
    te = jnp.minimum(jnp.sum((g[:, None] >= tile_end[None, :]).astype(jnp.int32), axis=1),
                     N_EXPERTS - 1)
    r = jnp.arange(tm, dtype=jnp.int32)
    start = (g - tile_off[te]) * tm
    nv = jnp.where(g < total, jnp.clip(counts[te] - start, 0, tm), 0).astype(jnp.int32)
    local = start[:, None] + r[None, :]
    valid = r[None, :] < nv[:, None]
    first = jnp.minimum(off[te] + start, n_rows)
    order_tail = jnp.concatenate([order, jnp.zeros((tm,), jnp.int32)])
    f = order_tail[first[:, None] + r[None, :]]
    tok = (jnp.where(valid, f % t, 0) * ROW_CHUNKS).astype(jnp.int32)
    dst = (jnp.where(valid, f, n_rows + r[None, :]) * ROW_CHUNKS).astype(jnp.int32)

    def lagged(a, lag):
        return jnp.pad(a, ((lag, _MOE_LAG - lag),) + ((0, 0),) * (a.ndim - 1))

    te_step = jnp.pad(te, (1, _MOE_LAG - 1), mode="edge")
    fresh = jnp.concatenate([jnp.ones((2,), jnp.int32),
                             (te_step[2:] != te_step[1:-1]).astype(jnp.int32)])
    plan = jnp.stack([te_step, lagged(nv, 0), lagged(nv, 1), lagged(nv, 2), lagged(nv, 3), fresh])
    n_steps = n_tiles + _MOE_LAG
    return (plan, lagged(tok, 0).reshape(n_steps, 1, tm), lagged(dst, 2).reshape(n_steps, 1, tm))


def _combine_kernel(xn_ref, tw_ref, y0_ref, y1_ref, y2_ref, y3_ref, lng_ref, lnb_ref, o_ref):
    tw = tw_ref[...]
    f = None
    for k, y_ref in enumerate((y0_ref, y1_ref, y2_ref, y3_ref)):
        term = tw[:, k:k + 1] * _load_row_tiles(y_ref, tw.shape[0])
        f = term if f is None else f + term
    o_ref[...] = _layer_norm(DEEPNORM_ALPHA * xn_ref[...] + f, lng_ref[...], lnb_ref[...])


def _combine(xn, tw, yb, ln_g, ln_b, tm):
    t = xn.shape[0]
    nblk = t // tm
    ysp = lambda k: pl.BlockSpec((tm * ROW_CHUNKS, LANES), lambda i, k=k: (k * nblk + i, 0))
    return pl.pallas_call(
        _combine_kernel,
        out_shape=jax.ShapeDtypeStruct((t, D_MODEL), F32),
        grid=(nblk,),
        in_specs=[pl.BlockSpec((tm, D_MODEL), lambda i: (i, 0)),
                  pl.BlockSpec((tm, LANES), lambda i: (i, 0)),
                  ysp(0), ysp(1), ysp(2), ysp(3),
                  pl.BlockSpec((1, D_MODEL), lambda i: (0, 0)),
                  pl.BlockSpec((1, D_MODEL), lambda i: (0, 0))],
        out_specs=pl.BlockSpec((tm, D_MODEL), lambda i: (i, 0)),
        compiler_params=_cparams(("parallel",)),
        name="combine_ln",
    )(xn, tw, yb, yb, yb, yb, ln_g, ln_b)


def kernel(x, mem, w_in, b_gate, diff_lambda, diff_subln_g, rel_bias, w_mem_kv, w_branch, w_out,
           ln1_g, ln1_b, router_w, router_b, w_gate_up, b_gate_up, w_down, b_down, ln2_g, ln2_b):
    b, s, d = x.shape
    t = b * s
    m = mem.shape[1]
    tq = min(ATT_TILE, s)
    tok_tile = min(TOK_TILE, t)

    idx = jnp.arange(tq, dtype=jnp.int32)
    ut = (idx[None, :] >= idx[:, None]).astype(BF16)
    bias = _bias_tiles(rel_bias, tq)
    mem2 = mem.reshape(b * m, d)
    xt = x.reshape(t, d)

    for l in range(DEPTH):
        w_in_l = jnp.concatenate([w_in[l][:, 7 * BRANCH_WIDTH:], w_in[l][:, :7 * BRANCH_WIDTH]],
                                 axis=1).astype(BF16)
        proj, sbv_t, dfv_t = _in_proj(xt, w_in_l, b, s, min(512, s), tq)
        proj3 = proj.reshape(b, s, IN_WIDTH)
        mem_kv = _matmul(mem2, w_mem_kv[l].astype(BF16), min(1024, b * m), 512, "mem_kv")
        mem_kv = mem_kv.reshape(b, m, 2 * BRANCH_WIDTH)

        lam_init = 0.8 - 0.6 * math.exp(-0.3 * l)
        ysb = _sb_attention(proj3, sbv_t, ut, tq)
        ydf = _df_attention(proj3, dfv_t, diff_lambda[l].astype(F32),
                            diff_subln_g[l].reshape(1, 2 * DIFF_HEAD_DIM).astype(F32),
                            bias, tq, lam_init)
        ymem = _mem_attention(proj3, mem_kv, min(512, s))

        rw = router_w[l].astype(F32)
        rwh = rw.astype(BF16)
        rwl = (rw - rwh.astype(F32)).astype(BF16)
        xn, xg, tw, ti = _merge(
            xt, proj, ysb.reshape(t, BRANCH_WIDTH), ydf.reshape(t, BRANCH_WIDTH),
            ymem.reshape(t, BRANCH_WIDTH), b_gate[l].reshape(N_BRANCHES, D_MODEL).astype(F32),
            w_branch[l].astype(BF16), w_out[l].astype(BF16),
            ln1_g[l].reshape(1, d).astype(F32), ln1_b[l].reshape(1, d).astype(F32),
            rwh, rwl, router_b[l].reshape(1, N_EXPERTS).astype(F32), tok_tile)

        plan, tok, dst = _route(ti, t, MOE_TILE)
        yb = _moe(xg, plan, tok, dst, w_gate_up[l],
                  b_gate_up[l].reshape(N_EXPERTS, 1, 2 * D_FF).astype(F32),
                  w_down[l], b_down[l].reshape(N_EXPERTS, 1, D_MODEL).astype(F32),
                  MOE_TILE)
        xt = _combine(xn, tw, yb, ln2_g[l].reshape(1, d).astype(F32),
                      ln2_b[l].reshape(1, d).astype(F32), tok_tile)
    return xt.reshape(b, s, d)
```

```python
import functools
import math

import jax
import jax.numpy as jnp
from jax import lax
from jax.experimental import pallas as pl
from jax.experimental.pallas import tpu as pltpu

F32 = jnp.float32
BF16 = jnp.bfloat16

D_MODEL = 1024
DEPTH = 2
CHUNK = 64
BRANCH_WIDTH = D_MODEL // 2
SB_HEAD_DIM = 64
DIFF_HEAD_DIM = 64
DIFF_HEADS = BRANCH_WIDTH // (2 * DIFF_HEAD_DIM)
MEM_HEAD_DIM = 128
MEM_HEADS = BRANCH_WIDTH // MEM_HEAD_DIM
N_BRANCHES = 3
GATE_WIDTH = N_BRANCHES * D_MODEL
IN_WIDTH = 7 * BRANCH_WIDTH + GATE_WIDTH
NUM_BUCKETS = 32
MAX_DISTANCE = 128
N_EXPERTS = 32
TOP_K = 4
D_FF = D_MODEL
SWIGLU_LIMIT = 7.0
SWIGLU_ALPHA = 1.702
LN_EPS = 1e-5
RMS_EPS = 1e-5
DEEPNORM_ALPHA = (2 * DEPTH) ** 0.25
LOG2E = math.log2(math.e)

LANES = 128
SUBLANES = 8
ROW_CHUNKS = D_MODEL // LANES
assert ROW_CHUNKS == SUBLANES
NEG_BIG = -1e30

_GATE_BLK = 0
_SBQ_BLK = GATE_WIDTH // LANES
_SBK_BLK = _SBQ_BLK + 4
_SBV_BLK = _SBQ_BLK + 8
_DFQ_BLK = _SBQ_BLK + 12
_DFK_BLK = _SBQ_BLK + 16
_DFV_BLK = _SBQ_BLK + 20
_MEMQ_BLK = _SBQ_BLK + 24

VMEM_LIMIT = 56 * 1024 * 1024

ATT_TILE = 256
MOE_TILE = 256
TOK_TILE = 512


def _cparams(sem):
    return pltpu.CompilerParams(dimension_semantics=sem, vmem_limit_bytes=VMEM_LIMIT)


def _load_row_tiles(ref, n, first=0, stride=ROW_CHUNKS):
    return jnp.concatenate([ref[pl.ds(first * ROW_CHUNKS + c, n, stride=stride), :]
                            for c in range(ROW_CHUNKS)], axis=1)


def _store_row_tiles(ref, value):
    n = value.shape[0]
    for c in range(ROW_CHUNKS):
        ref[pl.ds(c, n, stride=ROW_CHUNKS), :] = value[:, c * LANES:(c + 1) * LANES]


def _matmul_kernel(a_ref, w_ref, o_ref):
    a = a_ref[...].astype(BF16)
    o_ref[...] = jnp.dot(a, w_ref[...], preferred_element_type=F32).astype(o_ref.dtype)


def _matmul(a, w, tm, tn, name):
    m, k = a.shape
    n = w.shape[1]
    return pl.pallas_call(
        _matmul_kernel,
        out_shape=jax.ShapeDtypeStruct((m, n), BF16),
        grid=(m // tm, n // tn),
        in_specs=[pl.BlockSpec((tm, k), lambda i, j: (i, 0)),
                  pl.BlockSpec((k, tn), lambda i, j: (0, j))],
        out_specs=pl.BlockSpec((tm, tn), lambda i, j: (i, j)),
        compiler_params=_cparams(("parallel", "arbitrary")),
        name=name,
    )(a, w)


_IN_PROJ_CHUNKS = 4


def _in_proj_kernel(a_ref, w_ref, o_ref, sbvt_ref, dfvt_ref, *, tq):
    tm = a_ref.shape[0]
    a = a_ref[...].astype(BF16)
    cw = IN_WIDTH // _IN_PROJ_CHUNKS
    for c in range(_IN_PROJ_CHUNKS):
        o_ref[:, c * cw:(c + 1) * cw] = jnp.dot(
            a, w_ref[:, c * cw:(c + 1) * cw], preferred_element_type=F32).astype(o_ref.dtype)
    for vt_ref, blk in ((sbvt_ref, _SBV_BLK), (dfvt_ref, _DFV_BLK)):
        for sub in range(tm // tq):
            v = o_ref[sub * tq:(sub + 1) * tq, blk * LANES:blk * LANES + BRANCH_WIDTH]
            vt_ref[0, sub] = v.astype(F32).T.astype(vt_ref.dtype)


def _in_proj(x, w, b, s, tm, tq):
    t, k = x.shape
    per_batch = s // tm
    vt_shape = jax.ShapeDtypeStruct((b, s // tq, BRANCH_WIDTH, tq), BF16)
    vt_spec = pl.BlockSpec((1, tm // tq, BRANCH_WIDTH, tq),
                           lambda i: (i // per_batch, i % per_batch, 0, 0))
    return pl.pallas_call(
        functools.partial(_in_proj_kernel, tq=tq),
        out_shape=(jax.ShapeDtypeStruct((t, IN_WIDTH), BF16), vt_shape, vt_shape),
        grid=(t // tm,),
        in_specs=[pl.BlockSpec((tm, k), lambda i: (i, 0)),
                  pl.BlockSpec((k, IN_WIDTH), lambda i: (0, 0), pipeline_mode=pl.Buffered(1))],
        out_specs=(pl.BlockSpec((tm, IN_WIDTH), lambda i: (i, 0)), vt_spec, vt_spec),
        compiler_params=_cparams(("parallel",)),
        name="in_proj",
    )(x, w)


_NT = (((1,), (1,)), ((), ()))


def _sb_kernel(q_ref, k_ref, vt_ref, ut_ref, o_ref, c_ref, acc_ref, zn_ref, a_ref, *, tq):
    i = pl.program_id(1)
    n_heads = 2 * (BRANCH_WIDTH // LANES)
    low = lax.broadcasted_iota(jnp.int32, (tq, LANES), 1) < SB_HEAD_DIM
    q_heads = []
    for hp in range(n_heads // 2):
        qs = q_ref[0, :, hp * LANES:(hp + 1) * LANES].astype(F32) * (-LOG2E * SB_HEAD_DIM ** -0.5)
        q_heads += [jnp.where(low, qs, 0.0).astype(BF16), jnp.where(low, 0.0, qs).astype(BF16)]
    ut = ut_ref[...]
    key = lax.broadcasted_iota(jnp.int32, (tq, tq), 0)
    qry = lax.broadcasted_iota(jnp.int32, (tq, tq), 1)
    earlier = key < qry

    c_ref[...] = jnp.zeros_like(c_ref)
    acc_ref[...] = jnp.zeros_like(acc_ref)

    def scores(j):
        start = pl.multiple_of(j * tq, tq)
        for h in range(n_heads):
            kb = k_ref[0, pl.ds(start, tq), (h // 2) * LANES:(h // 2 + 1) * LANES]
            zn_ref[h] = lax.dot_general(kb, q_heads[h], _NT, preferred_element_type=F32)

    def weights(masked):
        l1ms = []
        for h in range(n_heads):
            zn = zn_ref[h]
            neg_abs = lax.bitcast_convert_type(
                lax.bitcast_convert_type(zn, jnp.uint32) | jnp.uint32(0x80000000), F32)
            l1m = jnp.minimum(zn, 0.0) - jnp.log2(1.0 + jnp.exp2(neg_abs))
            if masked:
                l1m = jnp.where(earlier, l1m, 0.0)
            l1ms.append(l1m.astype(BF16))
        incls = [jnp.dot(ut, l1ms[h], preferred_element_type=F32) for h in range(n_heads)]
        for h in range(n_heads):
            c = c_ref[h:h + 1, :]
            a = jnp.exp2(incls[h] + c - zn_ref[h])
            if masked:
                a = jnp.where(earlier, a, 0.0)
            a_ref[h] = a.astype(BF16)
            c_ref[h:h + 1, :] = c + incls[h][0:1, :]

    def values(j):
        for h in range(n_heads):
            vtb = vt_ref[0, j, (h // 2) * LANES:(h // 2 + 1) * LANES, :]
            acc_ref[h] += jnp.dot(vtb, a_ref[h], preferred_element_type=F32)

    scores(i)
    weights(True)
    scores(jnp.maximum(i - 1, 0))

    def body(n, carry):
        j = i - 1 - n
        values(j + 1)
        weights(False)
        scores(jnp.maximum(j - 1, 0))
        return carry

    lax.fori_loop(0, i, body, 0)
    values(0)
    sub = lax.broadcasted_iota(jnp.int32, (LANES, tq), 0)
    for hp in range(n_heads // 2):
        ot = jnp.where(sub < SB_HEAD_DIM, acc_ref[2 * hp], acc_ref[2 * hp + 1])
        o_ref[0, :, hp * LANES:(hp + 1) * LANES] = ot.T.astype(o_ref.dtype)


def _sb_attention(proj, vt, ut, tq):
    b, s, _ = proj.shape
    n_heads = 2 * (BRANCH_WIDTH // LANES)
    wblk = BRANCH_WIDTH // LANES
    kern = functools.partial(_sb_kernel, tq=tq)
    return pl.pallas_call(
        kern,
        out_shape=jax.ShapeDtypeStruct((b, s, BRANCH_WIDTH), BF16),
        grid=(b, s // tq),
        in_specs=[pl.BlockSpec((1, tq, BRANCH_WIDTH), lambda bi, i: (bi, i, _SBQ_BLK // wblk)),
                  pl.BlockSpec((1, s, BRANCH_WIDTH), lambda bi, i: (bi, 0, _SBK_BLK // wblk)),
                  pl.BlockSpec((1, s // tq, BRANCH_WIDTH, tq), lambda bi, i: (bi, 0, 0, 0)),
                  pl.BlockSpec((tq, tq), lambda bi, i: (0, 0))],
        out_specs=pl.BlockSpec((1, tq, BRANCH_WIDTH), lambda bi, i: (bi, i, 0)),
        scratch_shapes=[pltpu.VMEM((n_heads, tq), F32),
                        pltpu.VMEM((n_heads, LANES, tq), F32),
                        pltpu.VMEM((n_heads, tq, tq), F32),
                        pltpu.VMEM((n_heads, tq, tq), BF16)],
        compiler_params=_cparams(("parallel", "arbitrary")),
        name="sb_attention",
    )(proj, proj, vt, ut)


def _df_kernel(lam_ref, g_ref, q_ref, k_ref, vt_ref, bias_ref, o_ref, mx_ref, den_ref, acc_ref,
               sc_ref, p_ref, *, tq, lam_init):
    i = pl.program_id(1)
    n_maps = 2 * DIFF_HEADS
    low = lax.broadcasted_iota(jnp.int32, (tq, LANES), 1) < DIFF_HEAD_DIM
    q_maps = []
    for h in range(DIFF_HEADS):
        qs = q_ref[0, :, h * LANES:(h + 1) * LANES].astype(F32) * (LOG2E * DIFF_HEAD_DIM ** -0.5)
        q_maps += [jnp.where(low, qs, 0.0).astype(BF16), jnp.where(low, 0.0, qs).astype(BF16)]

    mx_ref[...] = jnp.full_like(mx_ref, NEG_BIG)
    den_ref[...] = jnp.zeros_like(den_ref)
    acc_ref[...] = jnp.zeros_like(acc_ref)

    def scores(j):
        start = pl.multiple_of(j * tq, tq)
        for m in range(n_maps):
            kb = k_ref[0, pl.ds(start, tq), (m // 2) * LANES:(m // 2 + 1) * LANES]
            sc_ref[m] = lax.dot_general(kb, q_maps[m], _NT, preferred_element_type=F32)

    def values(j):
        return [jnp.dot(vt_ref[0, j, (m // 2) * LANES:(m // 2 + 1) * LANES, :], p_ref[m],
                        preferred_element_type=F32) for m in range(n_maps)]

    def softmax(pvs, kind, off=None):
        for m in range(n_maps):
            sc = sc_ref[m]
            if kind is not None:
                sc = sc + bias_ref[m, kind]
            if off is not None:
                sc = sc + off
            mx = mx_ref[m:m + 1, :]
            mx_new = jnp.maximum(mx, jnp.max(sc, axis=0, keepdims=True))
            alpha = jnp.exp2(mx - mx_new)
            p = jnp.exp2(sc - mx_new)
            den_ref[m:m + 1, :] = alpha * den_ref[m:m + 1, :] + jnp.sum(p, axis=0, keepdims=True)
            mx_ref[m:m + 1, :] = mx_new
            p_ref[m] = p.astype(BF16)
            if pvs is not None:
                acc_ref[m] = alpha * (acc_ref[m] + pvs[m])

    scores(i)
    softmax(None, 0)
    pvs = values(i)
    scores(jnp.maximum(i - 1, 0))
    softmax(pvs, 1, jnp.where(i == 0, NEG_BIG, 0.0).astype(F32))
    scores(jnp.maximum(i - 2, 0))

    def body(n, carry):
        j = i - 2 - n
        pvs = values(j + 1)
        softmax(pvs, None)
        scores(jnp.maximum(j - 1, 0))
        return carry

    lax.fori_loop(0, jnp.maximum(i - 1, 0), body, 0)
    pvs = values(0)
    for m in range(n_maps):
        acc_ref[m] += pvs[m]


    lf = lam_ref[...]
    lam = (jnp.exp(jnp.sum(lf[0:1] * lf[1:2], axis=-1, keepdims=True))
           - jnp.exp(jnp.sum(lf[2:3] * lf[3:4], axis=-1, keepdims=True)) + lam_init)
    for h in range(DIFF_HEADS):
        m0, m1 = 2 * h, 2 * h + 1
        o = (acc_ref[m0] / den_ref[m0:m0 + 1, :]
             - lam * (acc_ref[m1] / den_ref[m1:m1 + 1, :]))
        o = o * lax.rsqrt(jnp.mean(o * o, axis=0, keepdims=True) + RMS_EPS)
        o_ref[0, :, h * LANES:(h + 1) * LANES] = (
            o.T * g_ref[...] * (1.0 - lam_init)).astype(o_ref.dtype)


def _df_attention(proj, vt, lam_rows, subln_g, bias, tq, lam_init):
    b, s, _ = proj.shape
    n_maps = 2 * DIFF_HEADS
    wblk = BRANCH_WIDTH // LANES
    kern = functools.partial(_df_kernel, tq=tq, lam_init=lam_init)
    return pl.pallas_call(
        kern,
        out_shape=jax.ShapeDtypeStruct((b, s, BRANCH_WIDTH), BF16),
        grid=(b, s // tq),
        in_specs=[pl.BlockSpec((4, DIFF_HEAD_DIM), lambda bi, i: (0, 0)),
                  pl.BlockSpec((1, LANES), lambda bi, i: (0, 0)),
                  pl.BlockSpec((1, tq, BRANCH_WIDTH), lambda bi, i: (bi, i, _DFQ_BLK // wblk)),
                  pl.BlockSpec((1, s, BRANCH_WIDTH), lambda bi, i: (bi, 0, _DFK_BLK // wblk)),
                  pl.BlockSpec((1, s // tq, BRANCH_WIDTH, tq), lambda bi, i: (bi, 0, 0, 0)),
                  pl.BlockSpec((n_maps, 2, tq, tq), lambda bi, i: (0, 0, 0, 0))],
        out_specs=pl.BlockSpec((1, tq, BRANCH_WIDTH), lambda bi, i: (bi, i, 0)),
        scratch_shapes=[pltpu.VMEM((n_maps, tq), F32),
                        pltpu.VMEM((n_maps, tq), F32),
                        pltpu.VMEM((n_maps, LANES, tq), F32),
                        pltpu.VMEM((n_maps, tq, tq), F32),
                        pltpu.VMEM((n_maps, tq, tq), BF16)],
        compiler_params=_cparams(("parallel", "arbitrary")),
        name="diff_attention",
    )(lam_rows, subln_g, proj, proj, vt, bias)


def _t5_bucket(rel):
    half = NUM_BUCKETS // 2
    max_exact = half // 2
    n = jnp.abs(rel)
    nf = jnp.maximum(n, 1).astype(F32)
    large = max_exact + (jnp.log(nf / max_exact) / math.log(MAX_DISTANCE / max_exact)
                         * (half - max_exact)).astype(jnp.int32)
    large = jnp.minimum(large, half - 1)
    return jnp.where(rel > 0, half, 0) + jnp.where(n < max_exact, n, large)


def _bias_tiles(rel_bias, tq):
    table = rel_bias.astype(F32)
    r = jnp.arange(tq, dtype=jnp.int32)
    rel_diag = r[None, :] - r[:, None]
    far = table[_t5_bucket(jnp.full((), -(tq + 1), jnp.int32))]
    diag = (table[_t5_bucket(rel_diag)] - far) * LOG2E
    visible = (r[None, :] // CHUNK) <= (r[:, None] // CHUNK)
    diag = jnp.where(visible[..., None], diag, NEG_BIG)
    near = (table[_t5_bucket(rel_diag - tq)] - far) * LOG2E
    return jnp.stack([diag, near], axis=0).transpose(3, 0, 2, 1)


def _mem_kernel(q_ref, kv_ref, o_ref):
    for h in range(MEM_HEADS):
        lo = h * MEM_HEAD_DIM
        q = (q_ref[0, :, lo:lo + MEM_HEAD_DIM].astype(F32) * (MEM_HEAD_DIM ** -0.5)).astype(BF16)
        k = kv_ref[0, :, lo:lo + MEM_HEAD_DIM]
        v = kv_ref[0, :, BRANCH_WIDTH + lo:BRANCH_WIDTH + lo + MEM_HEAD_DIM]
        sc = lax.dot_general(q, k, (((1,), (1,)), ((), ())), preferred_element_type=F32)
        p = jnp.exp(sc - jnp.max(sc, axis=-1, keepdims=True))
        den = jnp.sum(p, axis=-1, keepdims=True)
        o = jnp.dot(p.astype(BF16), v, preferred_element_type=F32) / den
        o_ref[0, :, lo:lo + MEM_HEAD_DIM] = o.astype(o_ref.dtype)


def _mem_attention(proj, mem_kv, tq):
    b, s, _ = proj.shape
    m = mem_kv.shape[1]
    return pl.pallas_call(
        _mem_kernel,
        out_shape=jax.ShapeDtypeStruct((b, s, BRANCH_WIDTH), BF16),
        grid=(b, s // tq),
        in_specs=[pl.BlockSpec((1, tq, BRANCH_WIDTH),
                               lambda bi, i: (bi, i, _MEMQ_BLK * LANES // BRANCH_WIDTH)),
                  pl.BlockSpec((1, m, 2 * BRANCH_WIDTH), lambda bi, i: (bi, 0, 0))],
        out_specs=pl.BlockSpec((1, tq, BRANCH_WIDTH), lambda bi, i: (bi, i, 0)),
        compiler_params=_cparams(("parallel", "arbitrary")),
        name="mem_attention",
    )(proj, mem_kv)


def _layer_norm(r, g, b):
    mu = jnp.mean(r, axis=-1, keepdims=True)
    rc = r - mu
    var = jnp.mean(rc * rc, axis=-1, keepdims=True)
    return rc * lax.rsqrt(var + LN_EPS) * g + b


def _merge_kernel(x_ref, g0_ref, g1_ref, g2_ref, ysb_ref, ydf_ref, ymem_ref, bg_ref, wb_ref,
                  wo_ref, lng_ref, lnb_ref, rwh_ref, rwl_ref, rb_ref, xn_ref, xg_ref, tw_ref, ti_ref):
    merged = None
    for br, (g_ref, y_ref) in enumerate(((g0_ref, ysb_ref), (g1_ref, ydf_ref), (g2_ref, ymem_ref))):
        gate = jax.nn.sigmoid(g_ref[...].astype(F32) + bg_ref[br:br + 1, :])
        term = gate * jnp.dot(y_ref[...], wb_ref[br], preferred_element_type=F32)
        merged = term if merged is None else merged + term
    h = jnp.dot(merged.astype(BF16), wo_ref[...], preferred_element_type=F32)
    xn = _layer_norm(DEEPNORM_ALPHA * x_ref[...].astype(F32) + h, lng_ref[...], lnb_ref[...])
    xn_ref[...] = xn
    _store_row_tiles(xg_ref, xn)

    xh = xn.astype(BF16)
    xl = (xn - xh.astype(F32)).astype(BF16)
    logits = (jnp.dot(xh, rwh_ref[...], preferred_element_type=F32)
              + jnp.dot(xl, rwh_ref[...], preferred_element_type=F32)
              + jnp.dot(xh, rwl_ref[...], preferred_element_type=F32)) + rb_ref[...]
    tm = logits.shape[0]
    eid = lax.broadcasted_iota(jnp.int32, (tm, N_EXPERTS), 1).astype(F32)
    cur = logits
    vals, idxs = [], []
    for _ in range(TOP_K):
        best = jnp.max(cur, axis=-1, keepdims=True)
        idx = jnp.min(jnp.where(cur == best, eid, float(N_EXPERTS)), axis=-1, keepdims=True)
        vals.append(best)
        idxs.append(idx)
        cur = jnp.where(eid == idx, -jnp.inf, cur)
    exps = [jnp.exp(v - vals[0]) for v in vals]
    den = exps[0] + exps[1] + exps[2] + exps[3]
    lane = lax.broadcasted_iota(jnp.int32, (tm, LANES), 1)
    tw = jnp.zeros((tm, LANES), F32)
    ti = jnp.zeros((tm, LANES), F32)
    for k in range(TOP_K):
        tw = jnp.where(lane == k, exps[k] / den, tw)
        ti = jnp.where(lane == k, idxs[k], ti)
    tw_ref[...] = tw
    ti_ref[...] = ti.astype(jnp.int32)


def _merge(x, proj, ysb, ydf, ymem, b_gate, wb, wo, ln_g, ln_b, rwh, rwl, rb, tm):
    t = x.shape[0]
    tok = lambda blk: pl.BlockSpec((tm, blk[0]), lambda i, c=blk[1]: (i, c))
    full2 = lambda shape: pl.BlockSpec(shape, lambda i: (0, 0))
    return pl.pallas_call(
        _merge_kernel,
        out_shape=(jax.ShapeDtypeStruct((t, D_MODEL), F32),
                   jax.ShapeDtypeStruct((t * ROW_CHUNKS, LANES), F32),
                   jax.ShapeDtypeStruct((t, LANES), F32),
                   jax.ShapeDtypeStruct((t, LANES), jnp.int32)),
        grid=(t // tm,),
        in_specs=[tok((D_MODEL, 0)),
                  tok((D_MODEL, 0)), tok((D_MODEL, 1)), tok((D_MODEL, 2)),
                  tok((BRANCH_WIDTH, 0)), tok((BRANCH_WIDTH, 0)), tok((BRANCH_WIDTH, 0)),
                  full2((N_BRANCHES, D_MODEL)),
                  pl.BlockSpec((N_BRANCHES, BRANCH_WIDTH, D_MODEL), lambda i: (0, 0, 0)),
                  full2((D_MODEL, D_MODEL)),
                  full2((1, D_MODEL)), full2((1, D_MODEL)),
                  full2((D_MODEL, N_EXPERTS)), full2((D_MODEL, N_EXPERTS)),
                  full2((1, N_EXPERTS))],
        out_specs=(pl.BlockSpec((tm, D_MODEL), lambda i: (i, 0)),
                   pl.BlockSpec((tm * ROW_CHUNKS, LANES), lambda i: (i, 0)),
                   pl.BlockSpec((tm, LANES), lambda i: (i, 0)),
                   pl.BlockSpec((tm, LANES), lambda i: (i, 0))),
        compiler_params=_cparams(("parallel",)),
        name="merge_ln_router",
    )(x, proj, proj, proj, ysb, ydf, ymem, b_gate, wb, wo, ln_g, ln_b, rwh, rwl, rb)


_MOE_LAG = 3


def _moe_kernel(plan_ref, tok_ref, dst_ref, x_hbm, wgu_ref, bgu_ref, wd_ref, bd_ref,
                y_hbm, xbuf0, xbuf1, ybuf0, ybuf1, wgu_bf, wd_bf, sem_in, sem_out,
                *, tm, n_rows):
    s = pl.program_id(0)
    do_gather = plan_ref[1, s] > 0
    do_compute = plan_ref[2, s] > 0
    do_scatter = plan_ref[3, s] > 0
    do_drain = plan_ref[4, s] > 0
    steady = do_gather & do_compute & do_scatter & do_drain
    parity = s & 1

    @pl.when(plan_ref[5, s] > 0)
    def _():
        wgu_bf[...] = wgu_ref[0].astype(BF16)
        wd_bf[...] = wd_ref[0].astype(BF16)

    @pl.when(s == 0)
    def _():
        ybuf1[...] = jnp.zeros_like(ybuf1)
        init = pltpu.make_async_copy(ybuf1, y_hbm.at[pl.ds(n_rows * ROW_CHUNKS, tm * ROW_CHUNKS), :],
                                     sem_out.at[1])
        init.start()
        init.wait()

    def stages(par):
        x_new, y_old = (xbuf0, ybuf0) if par == 0 else (xbuf1, ybuf1)
        x_cur, y_cur = (xbuf1, ybuf1) if par == 0 else (xbuf0, ybuf0)
        new, cur = par, 1 - par

        def gather_row(r):
            t = pl.multiple_of(tok_ref[0, 0, r], ROW_CHUNKS)
            pltpu.make_async_copy(x_hbm.at[pl.ds(t, ROW_CHUNKS), :],
                                  x_new.at[pl.ds(r * ROW_CHUNKS, ROW_CHUNKS), :],
                                  sem_in.at[new]).start(priority=0)

        def scatter_row(r):
            d = pl.multiple_of(dst_ref[0, 0, r], ROW_CHUNKS)
            pltpu.make_async_copy(y_old.at[pl.ds(r * ROW_CHUNKS, ROW_CHUNKS), :],
                                  y_hbm.at[pl.ds(d, ROW_CHUNKS), :], sem_out.at[new]).start(priority=1)

        def wait_gather():
            pltpu.make_async_copy(x_hbm.at[pl.ds(0, tm * ROW_CHUNKS), :], x_cur, sem_in.at[cur]).wait()

        def wait_scatter():
            pltpu.make_async_copy(y_cur, y_hbm.at[pl.ds(0, tm * ROW_CHUNKS), :], sem_out.at[cur]).wait()

        def compute():
            xb = _load_row_tiles(x_cur, tm).astype(BF16)
            h = jnp.dot(xb, wgu_bf[...], preferred_element_type=F32) + bgu_ref[0]
            gate = jnp.minimum(h[:, :D_FF], SWIGLU_LIMIT)
            up = jnp.clip(h[:, D_FF:], -SWIGLU_LIMIT, SWIGLU_LIMIT)
            act = (up + 1.0) * (gate * jax.nn.sigmoid(SWIGLU_ALPHA * gate))
            _store_row_tiles(
                y_cur, jnp.dot(act.astype(BF16), wd_bf[...], preferred_element_type=F32) + bd_ref[0])

        return gather_row, scatter_row, wait_gather, wait_scatter, compute

    for par in range(2):
        gather_row, scatter_row, wait_gather, wait_scatter, compute = stages(par)
        fast = steady & (parity == par)

        @pl.when(fast)
        def _():
            for r in range(tm):
                gather_row(r)
                scatter_row(r)

        @pl.when(fast & (plan_ref[1, s] > -1))
        def _():
            wait_gather()
            wait_scatter()
            compute()

        @pl.when(jnp.logical_not(steady) & (parity == par))
        def _():
            pl.when(do_compute)(wait_gather)
            pl.when(do_drain)(wait_scatter)

            @pl.when(do_gather)
            def _():
                lax.fori_loop(0, tm, lambda r, c: (gather_row(r), c)[1], 0)

            @pl.when(do_scatter)
            def _():
                lax.fori_loop(0, tm, lambda r, c: (scatter_row(r), c)[1], 0)

            pl.when(do_compute)(compute)


def _moe(xg, plan, tok, dst, wgu, bgu, wd, bd, tm):
    t = xg.shape[0] // ROW_CHUNKS
    n_steps = tok.shape[0]
    n_rows = TOP_K * t
    kern = functools.partial(_moe_kernel, tm=tm, n_rows=n_rows)
    grid_spec = pltpu.PrefetchScalarGridSpec(
        num_scalar_prefetch=1,
        grid=(n_steps,),
        in_specs=[pl.BlockSpec((1, 1, tm), lambda s, plan: (s, 0, 0), memory_space=pltpu.SMEM),
                  pl.BlockSpec((1, 1, tm), lambda s, plan: (s, 0, 0), memory_space=pltpu.SMEM),
                  pl.BlockSpec(memory_space=pl.ANY),
                  pl.BlockSpec((1, D_MODEL, 2 * D_FF), lambda s, plan: (plan[0, s], 0, 0)),
                  pl.BlockSpec((1, 1, 2 * D_FF), lambda s, plan: (plan[0, s], 0, 0)),
                  pl.BlockSpec((1, D_FF, D_MODEL), lambda s, plan: (plan[0, s], 0, 0)),
                  pl.BlockSpec((1, 1, D_MODEL), lambda s, plan: (plan[0, s], 0, 0))],
        out_specs=pl.BlockSpec(memory_space=pl.ANY),
        scratch_shapes=[pltpu.VMEM((tm * ROW_CHUNKS, LANES), F32),
                        pltpu.VMEM((tm * ROW_CHUNKS, LANES), F32),
                        pltpu.VMEM((tm * ROW_CHUNKS, LANES), F32),
                        pltpu.VMEM((tm * ROW_CHUNKS, LANES), F32),
                        pltpu.VMEM((D_MODEL, 2 * D_FF), BF16),
                        pltpu.VMEM((D_FF, D_MODEL), BF16),
                        pltpu.SemaphoreType.DMA((2,)),
                        pltpu.SemaphoreType.DMA((2,))],
    )
    return pl.pallas_call(
        kern,
        out_shape=jax.ShapeDtypeStruct(((n_rows + tm) * ROW_CHUNKS, LANES), F32),
        grid_spec=grid_spec,
        compiler_params=_cparams(("arbitrary",)),
        name="moe_experts",
    )(plan, tok, dst, xg, wgu, bgu, wd, bd)


def _route(topi, t, tm):
    n_rows = TOP_K * t
    e_flat = topi[:, :TOP_K].reshape(-1)
    order = jnp.argsort(e_flat, stable=True).astype(jnp.int32)
    experts = jnp.arange(N_EXPERTS, dtype=jnp.int32)
    counts = jnp.sum((e_flat[:, None] == experts[None, :]).astype(jnp.int32), axis=0)
    off = jnp.cumsum(counts) - counts
    tiles = (counts + tm - 1) // tm
    tile_end = jnp.cumsum(tiles)
    tile_off = tile_end - tiles
    total = tile_end[-1]
    n_tiles = n_rows // tm + N_EXPERTS
    g = jnp.arange(n_tiles, dtype=jnp.int32)
    te = jnp.minimum(jnp.sum((g[:, None] >= tile_end[None, :]).astype(jnp.int32), axis=1),
                     N_EXPERTS - 1)
    r = jnp.arange(tm, dtype=jnp.int32)
    start = (g - tile_off[te]) * tm
    nv = jnp.where(g < total, jnp.clip(counts[te] - start, 0, tm), 0).astype(jnp.int32)
    local = start[:, None] + r[None, :]
    valid = r[None, :] < nv[:, None]
    first = jnp.minimum(off[te] + start, n_rows)
    order_tail = jnp.concatenate([order, jnp.zeros((tm,), jnp.int32)])
    f = order_tail[first[:, None] + r[None, :]]
    tok = (jnp.where(valid, f // TOP_K, 0) * ROW_CHUNKS).astype(jnp.int32)
    dst = (jnp.where(valid, f, n_rows + r[None, :]) * ROW_CHUNKS).astype(jnp.int32)

    def lagged(a, lag):
        return jnp.pad(a, ((lag, _MOE_LAG - lag),) + ((0, 0),) * (a.ndim - 1))

    te_step = jnp.pad(te, (1, _MOE_LAG - 1), mode="edge")
    fresh = jnp.concatenate([jnp.ones((2,), jnp.int32),
                             (te_step[2:] != te_step[1:-1]).astype(jnp.int32)])
    plan = jnp.stack([te_step, lagged(nv, 0), lagged(nv, 1), lagged(nv, 2), lagged(nv, 3), fresh])
    n_steps = n_tiles + _MOE_LAG
    return (plan, lagged(tok, 0).reshape(n_steps, 1, tm), lagged(dst, 2).reshape(n_steps, 1, tm))


def _combine_kernel(xn_ref, tw_ref, y_ref, lng_ref, lnb_ref, o_ref):
    tw = tw_ref[...]
    f = None
    for k in range(TOP_K):
        term = tw[:, k:k + 1] * _load_row_tiles(y_ref, tw.shape[0], first=k,
                                                stride=TOP_K * ROW_CHUNKS)
        f = term if f is None else f + term
    o_ref[...] = _layer_norm(DEEPNORM_ALPHA * xn_ref[...] + f, lng_ref[...], lnb_ref[...])


def _combine(xn, tw, yb, ln_g, ln_b, tm):
    t = xn.shape[0]
    nblk = t // tm
    return pl.pallas_call(
        _combine_kernel,
        out_shape=jax.ShapeDtypeStruct((t, D_MODEL), F32),
        grid=(nblk,),
        in_specs=[pl.BlockSpec((tm, D_MODEL), lambda i: (i, 0)),
                  pl.BlockSpec((tm, LANES), lambda i: (i, 0)),
                  pl.BlockSpec((tm * TOP_K * ROW_CHUNKS, LANES), lambda i: (i, 0)),
                  pl.BlockSpec((1, D_MODEL), lambda i: (0, 0)),
                  pl.BlockSpec((1, D_MODEL), lambda i: (0, 0))],
        out_specs=pl.BlockSpec((tm, D_MODEL), lambda i: (i, 0)),
        compiler_params=_cparams(("parallel",)),
        name="combine_ln",
    )(xn, tw, yb, ln_g, ln_b)


def kernel(x, mem, w_in, b_gate, diff_lambda, diff_subln_g, rel_bias, w_mem_kv, w_branch, w_out,
           ln1_g, ln1_b, router_w, router_b, w_gate_up, b_gate_up, w_down, b_down, ln2_g, ln2_b):
    b, s, d = x.shape
    t = b * s
    m = mem.shape[1]
    tq = min(ATT_TILE, s)
    tok_tile = min(TOK_TILE, t)

    idx = jnp.arange(tq, dtype=jnp.int32)
    ut = (idx[None, :] >= idx[:, None]).astype(BF16)
    bias = _bias_tiles(rel_bias, tq)
    mem2 = mem.reshape(b * m, d)
    xt = x.reshape(t, d)

    for l in range(DEPTH):
        w_in_l = jnp.concatenate([w_in[l][:, 7 * BRANCH_WIDTH:], w_in[l][:, :7 * BRANCH_WIDTH]],
                                 axis=1).astype(BF16)
        proj, sbv_t, dfv_t = _in_proj(xt, w_in_l, b, s, min(512, s), tq)
        proj3 = proj.reshape(b, s, IN_WIDTH)
        mem_kv = _matmul(mem2, w_mem_kv[l].astype(BF16), min(1024, b * m), 512, "mem_kv")
        mem_kv = mem_kv.reshape(b, m, 2 * BRANCH_WIDTH)

        lam_init = 0.8 - 0.6 * math.exp(-0.3 * l)
        ysb = _sb_attention(proj3, sbv_t, ut, tq)
        ydf = _df_attention(proj3, dfv_t, diff_lambda[l].astype(F32),
                            diff_subln_g[l].reshape(1, 2 * DIFF_HEAD_DIM).astype(F32),
                            bias, tq, lam_init)
        ymem = _mem_attention(proj3, mem_kv, min(512, s))

        rw = router_w[l].astype(F32)
        rwh = rw.astype(BF16)
        rwl = (rw - rwh.astype(F32)).astype(BF16)
        xn, xg, tw, ti = _merge(
            xt, proj, ysb.reshape(t, BRANCH_WIDTH), ydf.reshape(t, BRANCH_WIDTH),
            ymem.reshape(t, BRANCH_WIDTH), b_gate[l].reshape(N_BRANCHES, D_MODEL).astype(F32),
            w_branch[l].astype(BF16), w_out[l].astype(BF16),
            ln1_g[l].reshape(1, d).astype(F32), ln1_b[l].reshape(1, d).astype(F32),
            rwh, rwl, router_b[l].reshape(1, N_EXPERTS).astype(F32), tok_tile)

        plan, tok, dst = _route(ti, t, MOE_TILE)
        yb = _moe(xg, plan, tok, dst, w_gate_up[l],
                  b_gate_up[l].reshape(N_EXPERTS, 1, 2 * D_FF).astype(F32),
                  w_down[l], b_down[l].reshape(N_EXPERTS, 1, D_MODEL).astype(F32),
                  MOE_TILE)
        xt = _combine(xn, tw, yb, ln2_g[l].reshape(1, d).astype(F32),
                      ln2_b[l].reshape(1, d).astype(F32), tok_tile)
    return xt.reshape(b, s, d)
```
